```python
import math
import jax, jax.numpy as jnp
from jax import lax
import numpy as np

D_MODEL = 1024
BATCH = 2
SEQ = 8192
DEPTH = 1

EPS = 1e-6
PLE_DIM = 256
ATTN_HEADS = 16
HEAD_DIM = 64
ATTN_WIDTH = ATTN_HEADS * HEAD_DIM
MOBA_BLOCK = 256
MOBA_TOPK = 3
Q_CHUNK = 32
SSM_INNER = 2 * D_MODEL
SSM_HEADDIM = 64
SSM_HEADS = SSM_INNER // SSM_HEADDIM
SSM_GROUPS = 4
SSM_STATE = 128
SSM_CONV = 4
SSM_CHUNK = 128
CONV_DIM = SSM_INNER + 2 * SSM_GROUPS * SSM_STATE
DT_MIN = 0.001
DT_MAX = 0.1
D_FF = -(-8 * D_MODEL // (3 * 256)) * 256
IN_SPLITS = (ATTN_WIDTH, ATTN_WIDTH, ATTN_WIDTH, SSM_INNER, CONV_DIM, SSM_HEADS, D_MODEL, D_MODEL)
IN_DIM = sum(IN_SPLITS)
IN_OFFSETS = tuple(np.cumsum(IN_SPLITS)[:-1].tolist())

kernel_name = "hybrid_moba_mamba2_gated_block"


def rms_norm(x, g):
    xf = x.astype(jnp.float32)
    y = xf * lax.rsqrt(jnp.mean(xf * xf, axis=-1, keepdims=True) + EPS)
    return (y * g.astype(jnp.float32)).astype(x.dtype)


def split_heads(t, b_, s):
    return t.reshape(b_, s, ATTN_HEADS, HEAD_DIM).transpose(0, 2, 1, 3)


def moba_attention(q, k, v):
    b_, nh, s, dh = q.shape
    nb = -(-s // MOBA_BLOCK)
    pad = nb * MOBA_BLOCK - s
    kp = jnp.pad(k, ((0, 0), (0, 0), (0, pad), (0, 0)))
    vp = jnp.pad(v, ((0, 0), (0, 0), (0, pad), (0, 0)))
    kb = kp.reshape(b_, nh, nb, MOBA_BLOCK, dh)
    vb = vp.reshape(b_, nh, nb, MOBA_BLOCK, dh)
    k_mean = jnp.mean(kb.astype(jnp.float32), axis=3)
    topk = min(MOBA_TOPK, nb)
    scale = dh ** -0.5
    bidx = jnp.arange(b_)[:, None, None, None]
    hidx = jnp.arange(nh)[None, :, None, None]
    blk_ids = jnp.arange(nb)

    def chunk(ci):
        start = ci * Q_CHUNK
        qc = lax.dynamic_slice_in_dim(q, start, Q_CHUNK, axis=2)
        qpos = start + jnp.arange(Q_CHUNK)
        own = start // MOBA_BLOCK
        gate = jnp.einsum('bhqd,bhnd->bhqn', qc.astype(jnp.float32), k_mean)
        gate = jnp.where(blk_ids < own, gate, -jnp.inf)
        _, sel = lax.top_k(gate, topk)
        sel_ok = sel < own
        ks = kb[bidx, hidx, sel]
        vs = vb[bidx, hidx, sel]
        s_sel = jnp.einsum('bhqd,bhqtkd->bhqtk', qc, ks, preferred_element_type=jnp.float32) * scale
        s_sel = jnp.where(sel_ok[..., None], s_sel, -jnp.inf)
        ko = lax.dynamic_slice_in_dim(kp, own * MOBA_BLOCK, MOBA_BLOCK, axis=2)
        vo = lax.dynamic_slice_in_dim(vp, own * MOBA_BLOCK, MOBA_BLOCK, axis=2)
        s_own = jnp.einsum('bhqd,bhkd->bhqk', qc, ko, preferred_element_type=jnp.float32) * scale
        kpos = own * MOBA_BLOCK + jnp.arange(MOBA_BLOCK)
        s_own = jnp.where(kpos[None, :] <= qpos[:, None], s_own, -jnp.inf)
        scores = jnp.concatenate([s_sel.reshape(b_, nh, Q_CHUNK, topk * MOBA_BLOCK), s_own], axis=-1)
        probs = jax.nn.softmax(scores, axis=-1).astype(v.dtype)
        p_sel = probs[..., :topk * MOBA_BLOCK].reshape(b_, nh, Q_CHUNK, topk, MOBA_BLOCK)
        p_own = probs[..., topk * MOBA_BLOCK:]
        return (jnp.einsum('bhqtk,bhqtkd->bhqd', p_sel, vs)
                + jnp.einsum('bhqk,bhkd->bhqd', p_own, vo))

    out = lax.map(chunk, jnp.arange(s // Q_CHUNK))
    return out.transpose(1, 2, 0, 3, 4).reshape(b_, nh, s, dh)


def causal_depthwise_conv(u, w, b):
    ch = u.shape[-1]
    out = lax.conv_general_dilated(u, w[:, None, :].astype(u.dtype), (1,), [(SSM_CONV - 1, 0)],
                                   dimension_numbers=('NWC', 'WIO', 'NWC'), feature_group_count=ch)
    return out + b.astype(u.dtype)


def segsum(a):
    t = a.shape[-1]
    xr = jnp.broadcast_to(a[..., :, None], a.shape + (t,))
    xr = jnp.where(jnp.tril(jnp.ones((t, t), bool), -1), xr, 0.0)
    cs = jnp.cumsum(xr, axis=-2)
    return jnp.where(jnp.tril(jnp.ones((t, t), bool)), cs, -jnp.inf)


def ssd_chunked(xh, dt, a_head, bm, cm):
    b_, s, h, p = xh.shape
    g, n = bm.shape[-2], bm.shape[-1]
    j = h // g
    L = SSM_CHUNK
    c = s // L
    xdt = (xh * dt[..., None]).reshape(b_, c, L, g, j, p)
    a = (dt * a_head).reshape(b_, c, L, h).transpose(0, 3, 1, 2)
    a_cs = jnp.cumsum(a, axis=-1)
    bc = bm.reshape(b_, c, L, g, n)
    cc = cm.reshape(b_, c, L, g, n)
    decay_in = jnp.exp(segsum(a)).reshape(b_, g, j, c, L, L)
    cb = jnp.einsum('bclgn,bcsgn->bgcls', cc, bc)
    y_diag = jnp.einsum('bgjcls,bcsgjp->bclgjp', cb[:, :, None] * decay_in, xdt)
    w_end = jnp.exp(a_cs[..., -1:] - a_cs).transpose(0, 2, 3, 1).reshape(b_, c, L, g, j)
    chunk_states = jnp.einsum('bclgn,bclgjp->bcgjpn', bc, xdt * w_end[..., None])
    chunk_decay = jnp.exp(a_cs[..., -1]).reshape(b_, g, j, c).transpose(3, 0, 1, 2)

    def step(state, inp):
        dec, new = inp
        return state * dec[..., None, None] + new, state

    init = jnp.zeros((b_, g, j, p, n), jnp.float32)
    _, prev = lax.scan(step, init, (chunk_decay, chunk_states.transpose(1, 0, 2, 3, 4, 5)))
    prev = prev.transpose(1, 0, 2, 3, 4, 5)
    w_start = jnp.exp(a_cs).transpose(0, 2, 3, 1).reshape(b_, c, L, g, j)
    y_off = jnp.einsum('bclgn,bcgjpn->bclgjp', cc, prev) * w_start[..., None]
    return (y_diag + y_off).reshape(b_, s, h, p)


def mamba2_mixer(z, xbc, dt_raw, conv_w, conv_b, dt_bias, a_log, d_skip, norm_g):
    b_, s, _ = z.shape
    f32 = jnp.float32
    xbc = jax.nn.silu(causal_depthwise_conv(xbc, conv_w, conv_b))
    xs, bm, cm = jnp.split(xbc, [SSM_INNER, SSM_INNER + SSM_GROUPS * SSM_STATE], axis=-1)
    dt = jax.nn.softplus(dt_raw.astype(f32) + dt_bias.astype(f32))
    a_head = -jnp.exp(a_log.astype(f32))
    xh = xs.astype(f32).reshape(b_, s, SSM_HEADS, SSM_HEADDIM)
    y = ssd_chunked(xh, dt, a_head,
                    bm.astype(f32).reshape(b_, s, SSM_GROUPS, SSM_STATE),
                    cm.astype(f32).reshape(b_, s, SSM_GROUPS, SSM_STATE))
    y = y + d_skip.astype(f32)[:, None] * xh
    y = y.reshape(b_, s, SSM_INNER) * jax.nn.silu(z.astype(f32))
    y = y.reshape(b_, s, SSM_GROUPS, SSM_INNER // SSM_GROUPS)
    y = y * lax.rsqrt(jnp.mean(y * y, axis=-1, keepdims=True) + EPS)
    return (y.reshape(b_, s, SSM_INNER) * norm_g.astype(f32)).astype(z.dtype)


def setup_inputs(seed: int = 0) -> dict:
    key = jax.random.key(seed)
    ks = jax.random.split(key, 24)
    f32 = jnp.float32

    def normal(k, shape, scale):
        return jax.random.normal(k, shape, f32) * scale

    def gain(k, shape):
        return 1.0 + 0.02 * jax.random.normal(k, shape, f32)

    dt0 = jnp.exp(jax.random.uniform(ks[9], (DEPTH, SSM_HEADS), f32, math.log(DT_MIN), math.log(DT_MAX)))
    return {
        "x": normal(ks[0], (BATCH, SEQ, D_MODEL), 1.0),
        "p": normal(ks[1], (DEPTH, BATCH, SEQ, PLE_DIM), 1.0),
        "ln1_g": gain(ks[2], (DEPTH, D_MODEL)),
        "w_in": normal(ks[3], (DEPTH, D_MODEL, IN_DIM), D_MODEL ** -0.5),
        "q_norm_g": gain(ks[4], (DEPTH, HEAD_DIM)),
        "k_norm_g": gain(ks[5], (DEPTH, HEAD_DIM)),
        "w_o_attn": normal(ks[6], (DEPTH, ATTN_WIDTH, D_MODEL), ATTN_WIDTH ** -0.5),
        "conv_w": normal(ks[7], (DEPTH, SSM_CONV, CONV_DIM), SSM_CONV ** -0.5),
        "conv_b": normal(ks[8], (DEPTH, CONV_DIM), 0.02),
        "dt_bias": dt0 + jnp.log(-jnp.expm1(-dt0)),
        "a_log": jnp.log(jax.random.uniform(ks[10], (DEPTH, SSM_HEADS), f32, 1.0, 16.0)),
        "d_skip": gain(ks[11], (DEPTH, SSM_HEADS)),
        "ssm_norm_g": gain(ks[12], (DEPTH, SSM_INNER)),
        "w_o_ssm": normal(ks[13], (DEPTH, SSM_INNER, D_MODEL), SSM_INNER ** -0.5),
        "w_out": normal(ks[14], (DEPTH, D_MODEL, D_MODEL), D_MODEL ** -0.5),
        "ln2_g": gain(ks[15], (DEPTH, D_MODEL)),
        "w_gate_up": normal(ks[16], (DEPTH, D_MODEL, 2 * D_FF), D_MODEL ** -0.5),
        "w_down": normal(ks[17], (DEPTH, D_FF, D_MODEL), D_FF ** -0.5),
        "ln3_g": gain(ks[18], (DEPTH, D_MODEL)),
        "w_ple_gate": normal(ks[19], (DEPTH, D_MODEL, D_MODEL), D_MODEL ** -0.5),
        "w_ple_proj": normal(ks[20], (DEPTH, PLE_DIM, D_MODEL), PLE_DIM ** -0.5),
    }


def reference(x, p, ln1_g, w_in, q_norm_g, k_norm_g, w_o_attn, conv_w, conv_b, dt_bias, a_log,
              d_skip, ssm_norm_g, w_o_ssm, w_out, ln2_g, w_gate_up, w_down, ln3_g, w_ple_gate,
              w_ple_proj):
    b_, s, _ = x.shape
    for i in range(DEPTH):
        h = rms_norm(x, ln1_g[i])
        proj = h @ w_in[i]
        q, k, v, z, xbc, dt_raw, gate_a, gate_b = jnp.split(proj, IN_OFFSETS, axis=-1)
        qh = rms_norm(split_heads(q, b_, s), q_norm_g[i])
        kh = rms_norm(split_heads(k, b_, s), k_norm_g[i])
        vh = split_heads(v, b_, s)
        att = moba_attention(qh, kh, vh)
        y_a = att.transpose(0, 2, 1, 3).reshape(b_, s, ATTN_WIDTH) @ w_o_attn[i]
        y_b = mamba2_mixer(z, xbc, dt_raw, conv_w[i], conv_b[i], dt_bias[i], a_log[i],
                           d_skip[i], ssm_norm_g[i]) @ w_o_ssm[i]
        merged = jax.nn.sigmoid(gate_a) * y_a + jax.nn.sigmoid(gate_b) * y_b
        x = x + merged @ w_out[i]
        h2 = rms_norm(x, ln2_g[i])
        g_ff, u_ff = jnp.split(h2 @ w_gate_up[i], 2, axis=-1)
        x = x + (jax.nn.silu(g_ff) * u_ff) @ w_down[i]
        h3 = rms_norm(x, ln3_g[i])
        x = x + (p[i] @ w_ple_proj[i]) * jax.nn.sigmoid(h3 @ w_ple_gate[i])
    return x
```

```python
import functools

import jax
import jax.numpy as jnp
from jax import lax
from jax.experimental import pallas as pl
from jax.experimental.pallas import tpu as pltpu

F32 = jnp.float32
BF16 = jnp.bfloat16

EPS = 1e-6
LANES = 128
HEAD_DIM = 64
ATTN_HEADS = 16
HEAD_PAIRS = ATTN_HEADS // 2
MOBA_BLOCK = 256
MOBA_TOPK = 3
MAX_BLOCKS = 32
SSM_HEADS = 32
SSM_PAIRS = SSM_HEADS // 2
SSM_GROUPS = 4
SSM_STATE = 128
SSM_CONV = 4
SSM_CHUNK = 128
CONV_PAD = 8
NEG = -1e30
VMEM_LIMIT = 56 * 1024 * 1024


def _sigmoid(v):
    return 1.0 / (1.0 + jnp.exp(-v))


def _nt_dot(a, b, **kw):
    return lax.dot_general(a, b, (((1,), (1,)), ((), ())), preferred_element_type=F32, **kw)


def _inproj_kernel(x_ref, g_ref, w_ref, wdt_ref, o_ref, dt_ref, h_ref):
    @pl.when(pl.program_id(1) == 0)
    def _():
        x = x_ref[...]
        ms = jnp.mean(x * x, axis=-1, keepdims=True)
        h = (x * lax.rsqrt(ms + EPS) * g_ref[...]).astype(BF16)
        h_ref[...] = h
        dt_ref[...] = jnp.dot(h, wdt_ref[...], preferred_element_type=F32)

    o_ref[...] = jnp.dot(h_ref[...], w_ref[...], preferred_element_type=F32).astype(o_ref.dtype)


def _in_proj(x2d, g, w_main, w_dt, tm=1024, tn=1024):
    m, d = x2d.shape
    n = w_main.shape[1]
    return pl.pallas_call(
        _inproj_kernel,
        grid=(m // tm, n // tn),
        in_specs=[
            pl.BlockSpec((tm, d), lambda i, j: (i, 0)),
            pl.BlockSpec((1, d), lambda i, j: (0, 0)),
            pl.BlockSpec((d, tn), lambda i, j: (0, j)),
            pl.BlockSpec((d, LANES), lambda i, j: (0, 0)),
        ],
        out_specs=[
            pl.BlockSpec((tm, tn), lambda i, j: (i, j)),
            pl.BlockSpec((tm, LANES), lambda i, j: (i, 0)),
        ],
        out_shape=[jax.ShapeDtypeStruct((m, n), BF16), jax.ShapeDtypeStruct((m, LANES), F32)],
        scratch_shapes=[pltpu.VMEM((tm, d), BF16)],
        compiler_params=pltpu.CompilerParams(
            dimension_semantics=("parallel", "arbitrary"), vmem_limit_bytes=VMEM_LIMIT),
        name="in_proj",
    )(x2d, g, w_main, w_dt)


def _head_rms_norm(x, gain, is_a):
    outs = []
    for c in range(x.shape[1] // LANES):
        blk = x[:, c * LANES:(c + 1) * LANES]
        sq = blk * blk
        sa = jnp.sum(jnp.where(is_a, sq, 0.0), axis=-1, keepdims=True)
        sb = jnp.sum(jnp.where(is_a, 0.0, sq), axis=-1, keepdims=True)
        ms = jnp.where(is_a, sa, sb) * (1.0 / HEAD_DIM)
        outs.append(blk * lax.rsqrt(ms + EPS))
    return jnp.concatenate(outs, axis=1) * gain


def _prep_kernel(q_ref, k_ref, v_ref, qg_ref, kg_ref, qn_ref, kn_ref, ft_ref, vt_ref, kmean_ref):
    i = pl.program_id(1)

    @pl.when(i == 0)
    def _():
        kmean_ref[...] = jnp.zeros_like(kmean_ref)

    lane = lax.broadcasted_iota(jnp.int32, (1, LANES), 1)
    is_a = lane < HEAD_DIM
    qn = _head_rms_norm(q_ref[...].astype(F32), qg_ref[...], is_a) * (HEAD_DIM ** -0.5)
    kn = _head_rms_norm(k_ref[...].astype(F32), kg_ref[...], is_a)
    qn_ref[...] = qn.astype(BF16)
    kn_ref[...] = kn.astype(BF16)

    vt = v_ref[...].astype(F32).T
    vt_ref[...] = vt.astype(BF16).reshape(vt_ref.shape)

    kmean = kmean_ref[...]
    jidx = lax.broadcasted_iota(jnp.int32, (MAX_BLOCKS, MOBA_BLOCK), 0)
    valid = jidx < i
    zeros_rows = jnp.zeros((MAX_BLOCKS, MOBA_BLOCK), F32)
    for p in range(HEAD_PAIRS):
        qp = qn[:, p * LANES:(p + 1) * LANES]
        kmp = kmean[:, p * LANES:(p + 1) * LANES]
        bias = []
        for hh in range(2):
            head_lanes = is_a if hh == 0 else jnp.logical_not(is_a)
            km = jnp.where(head_lanes, kmp, 0.0)
            gate = _nt_dot(km, qp, precision=lax.Precision.HIGHEST)
            gate = jnp.where(valid, gate, -jnp.inf)
            sel = jnp.zeros_like(gate)
            for _ in range(MOBA_TOPK):
                top = jnp.max(gate, axis=0, keepdims=True)
                first = jnp.min(jnp.where(gate == top, jidx, MAX_BLOCKS), axis=0, keepdims=True)
                pick = jidx == first
                sel = jnp.where(pick, 1.0, sel)
                gate = jnp.where(pick, -jnp.inf, gate)
            keep = jnp.logical_or(jnp.logical_and(sel > 0.0, valid), jidx == i)
            bias.append(jnp.where(keep, 0.0, NEG))
        feat_t = jnp.concatenate([bias[1], zeros_rows, bias[0], zeros_rows], axis=0)
        ft_ref[:, p * LANES:(p + 1) * LANES] = feat_t.T.astype(BF16)

    kmean_ref[pl.ds(i, 1), :] = jnp.mean(kn, axis=0, keepdims=True)


def _attn_prep(proj3, qg, kg):
    b, s, _ = proj3.shape
    nb = s // MOBA_BLOCK
    width = ATTN_HEADS * HEAD_DIM
    tile = lambda col: pl.BlockSpec((None, MOBA_BLOCK, width), lambda bb, i, col=col: (bb, i, col))
    gain = pl.BlockSpec((1, width), lambda bb, i: (0, 0))
    return pl.pallas_call(
        _prep_kernel,
        grid=(b, nb),
        in_specs=[tile(0), tile(1), tile(2), gain, gain],
        out_specs=[
            tile(0), tile(0), tile(0),
            pl.BlockSpec((None, HEAD_PAIRS, None, LANES, MOBA_BLOCK), lambda bb, i: (bb, 0, i, 0, 0)),
        ],
        out_shape=[
            jax.ShapeDtypeStruct((b, s, width), BF16),
            jax.ShapeDtypeStruct((b, s, width), BF16),
            jax.ShapeDtypeStruct((b, s, width), BF16),
            jax.ShapeDtypeStruct((b, HEAD_PAIRS, nb, LANES, MOBA_BLOCK), BF16),
        ],
        scratch_shapes=[pltpu.VMEM((MAX_BLOCKS, width), F32)],
        compiler_params=pltpu.CompilerParams(
            dimension_semantics=("parallel", "arbitrary"), vmem_limit_bytes=VMEM_LIMIT),
        name="attn_prep",
    )(proj3, proj3, proj3, qg, kg)


def _attn_kernel(q_ref, f_ref, k_ref, vt_ref, o_ref):
    i = pl.program_id(2)
    lane = lax.broadcasted_iota(jnp.int32, (1, LANES), 1)
    is_a = lane < HEAD_DIM
    qn = q_ref[...]
    ft = f_ref[...]
    q_aug = (jnp.where(is_a, qn, ft), jnp.where(is_a, ft, qn))

    def scores(j):
        kt = k_ref[pl.ds(pl.multiple_of(j * MOBA_BLOCK, MOBA_BLOCK), MOBA_BLOCK), :]
        one_a = (lane == HEAD_DIM + j).astype(BF16)
        one_b = (lane == j).astype(BF16)
        k_aug = (jnp.where(is_a, kt, one_a), jnp.where(is_a, one_b, kt))
        return [_nt_dot(k_aug[hh], q_aug[hh]) for hh in range(2)]

    krow = lax.broadcasted_iota(jnp.int32, (MOBA_BLOCK, MOBA_BLOCK), 0)
    qcol = lax.broadcasted_iota(jnp.int32, (MOBA_BLOCK, MOBA_BLOCK), 1)
    causal = krow <= qcol
    vt = vt_ref[i]
    m0, l0, acc0 = [], [], []
    for hh, s in enumerate(scores(i)):
        s = jnp.where(causal, s, NEG)
        m = jnp.max(s, axis=0, keepdims=True)
        p = jnp.exp(s - m)
        m0.append(m)
        l0.append(jnp.sum(p, axis=0, keepdims=True))
        acc0.append(jnp.dot(vt[hh * HEAD_DIM:(hh + 1) * HEAD_DIM, :], p.astype(BF16),
                            preferred_element_type=F32))

    def body(j, carry):
        ms, ls, accs = carry
        vt = vt_ref[j]
        new_m, new_l, new_acc = [], [], []
        for hh, s in enumerate(scores(j)):
            m = jnp.maximum(ms[hh], jnp.max(s, axis=0, keepdims=True))
            p = jnp.exp(s - m)
            alpha = jnp.exp(ms[hh] - m)
            new_m.append(m)
            new_l.append(alpha * ls[hh] + jnp.sum(p, axis=0, keepdims=True))
            pv = jnp.dot(vt[hh * HEAD_DIM:(hh + 1) * HEAD_DIM, :], p.astype(BF16),
                         preferred_element_type=F32)
            new_acc.append(alpha * accs[hh] + pv)
        return tuple(new_m), tuple(new_l), tuple(new_acc)

    _, ls, accs = lax.fori_loop(0, i, body, (tuple(m0), tuple(l0), tuple(acc0)))
    out_t = jnp.concatenate([accs[0] * (1.0 / ls[0]), accs[1] * (1.0 / ls[1])], axis=0)
    o_ref[...] = out_t.T.astype(o_ref.dtype)


def _moba_attention(qn, kn, feat, vt5):
    b, s, width = qn.shape
    nb = s // MOBA_BLOCK
    qtile = pl.BlockSpec((None, MOBA_BLOCK, LANES), lambda bb, p, i: (bb, i, p))
    return pl.pallas_call(
        _attn_kernel,
        grid=(b, HEAD_PAIRS, nb),
        in_specs=[
            qtile, qtile,
            pl.BlockSpec((None, s, LANES), lambda bb, p, i: (bb, 0, p)),
            pl.BlockSpec((None, None, nb, LANES, MOBA_BLOCK), lambda bb, p, i: (bb, p, 0, 0, 0)),
        ],
        out_specs=qtile,
        out_shape=jax.ShapeDtypeStruct((b, s, width), BF16),
        compiler_params=pltpu.CompilerParams(
            dimension_semantics=("parallel", "parallel", "arbitrary"), vmem_limit_bytes=VMEM_LIMIT),
        name="moba_attn",
    )(qn, feat, kn, vt5)


def _ssd_kernel(xbc_ref, z_ref, dt_ref, cw_ref, cb_ref, dtb_ref, alog_ref, dsk_ref, ng_ref, o_ref,
                ext_ref, state_ref, xs_ref, y_ref, acst_ref, dtt_ref, cbm_ref, bt_ref, cm_ref):
    c = pl.program_id(1)
    L = SSM_CHUNK
    inner = SSM_HEADS * HEAD_DIM

    @pl.when(c == 0)
    def _():
        ext_ref[0:CONV_PAD, :] = jnp.zeros((CONV_PAD, ext_ref.shape[1]), F32)
        state_ref[...] = jnp.zeros_like(state_ref)

    ext_ref[CONV_PAD:CONV_PAD + L, :] = xbc_ref[...].astype(F32)
    conv = cb_ref[...]
    for k in range(SSM_CONV):
        start = CONV_PAD - (SSM_CONV - 1) + k
        conv = conv + cw_ref[k:k + 1, :] * ext_ref[start:start + L, :]
    ext_ref[0:CONV_PAD, :] = ext_ref[L:L + CONV_PAD, :]
    act = conv * _sigmoid(conv)
    xs = act[:, :inner]
    for j in range(SSM_PAIRS):
        xs_ref[j] = xs[:, j * LANES:(j + 1) * LANES]
    for g in range(SSM_GROUPS):
        bg = act[:, inner + g * SSM_STATE:inner + (g + 1) * SSM_STATE]
        cg = act[:, inner + (SSM_GROUPS + g) * SSM_STATE:inner + (SSM_GROUPS + g + 1) * SSM_STATE]
        cbm_ref[g] = _nt_dot(cg.astype(BF16), bg.astype(BF16))
        bt_ref[g] = bg.T
        cm_ref[g] = cg

    dt_in = dt_ref[...] + dtb_ref[...]
    dtv = jnp.maximum(dt_in, 0.0) + jnp.log(1.0 + jnp.exp(-jnp.abs(dt_in)))
    a = dtv * (-jnp.exp(alog_ref[...]))
    row = lax.broadcasted_iota(jnp.int32, (L, L), 0)
    col = lax.broadcasted_iota(jnp.int32, (L, L), 1)
    tri = col <= row
    acs = jnp.dot(tri.astype(F32), a, preferred_element_type=F32, precision=lax.Precision.HIGHEST)
    acst_ref[...] = acs.T
    dtt_ref[...] = dtv.T
    lane = lax.broadcasted_iota(jnp.int32, (1, LANES), 1)

    def pair_body(j, carry):
        g = j // (SSM_PAIRS // SSM_GROUPS)
        cbm = cbm_ref[g]
        bt = bt_ref[g]
        cg = cm_ref[g]
        xp = xs_ref[j].astype(BF16)
        prev = state_ref[j]
        rhs = jnp.concatenate([xp, prev.astype(BF16)], axis=0)
        y_pair = jnp.zeros((L, LANES), F32)
        s_new = jnp.zeros((SSM_STATE, LANES), F32)
        s_dec = jnp.zeros((1, LANES), F32)
        for hh in range(2):
            h = 2 * j + hh
            arow = acst_ref[pl.ds(h, 1), :]
            drow = dtt_ref[pl.ds(h, 1), :]
            rmat = jnp.broadcast_to(arow, (L, L))
            cmat = rmat.T
            decay = jnp.exp(jnp.where(tri, cmat - rmat, NEG))
            m_in = (decay * cbm * drow).astype(BF16)
            c_in = (cg * jnp.exp(cmat)).astype(BF16)
            y_full = jnp.dot(jnp.concatenate([m_in, c_in], axis=1), rhs, preferred_element_type=F32)
            tot = arow[:, L - 1:L]
            srow = drow * jnp.exp(tot - arow)
            s_full = jnp.dot((bt * srow).astype(BF16), xp, preferred_element_type=F32)
            half = (lane >= HEAD_DIM) if hh == 1 else (lane < HEAD_DIM)
            y_pair = jnp.where(half, y_full, y_pair)
            s_new = jnp.where(half, s_full, s_new)
            s_dec = jnp.where(half, jnp.exp(tot), s_dec)
        y_ref[j] = y_pair
        state_ref[j] = prev * s_dec + s_new
        return carry

    lax.fori_loop(0, SSM_PAIRS, pair_body, 0)

    y = jnp.concatenate([y_ref[j] for j in range(SSM_PAIRS)], axis=1)
    y = y + dsk_ref[...] * xs
    zf = z_ref[...].astype(F32)
    y = y * (zf * _sigmoid(zf))
    gw = inner // SSM_GROUPS
    outs = []
    for g in range(SSM_GROUPS):
        seg = y[:, g * gw:(g + 1) * gw]
        ms = jnp.mean(seg * seg, axis=-1, keepdims=True)
        outs.append(seg * lax.rsqrt(ms + EPS))
    o_ref[...] = (jnp.concatenate(outs, axis=1) * ng_ref[...]).astype(o_ref.dtype)


def _ssd(proj3, dt3, conv_w, conv_b, dt_bias, a_log, d_skip_x, norm_g):
    b, s, _ = proj3.shape
    L = SSM_CHUNK
    inner = SSM_HEADS * HEAD_DIM
    conv_dim = inner + 2 * SSM_GROUPS * SSM_STATE
    vec = lambda w: pl.BlockSpec((1, w), lambda bb, c: (0, 0))
    return pl.pallas_call(
        _ssd_kernel,
        grid=(b, s // L),
        in_specs=[
            pl.BlockSpec((None, L, conv_dim), lambda bb, c: (bb, c, 1)),
            pl.BlockSpec((None, L, inner), lambda bb, c: (bb, c, 3)),
            pl.BlockSpec((None, L, LANES), lambda bb, c: (bb, c, 0)),
            pl.BlockSpec((SSM_CONV, conv_dim), lambda bb, c: (0, 0)),
            vec(conv_dim), vec(LANES), vec(LANES), vec(inner), vec(inner),
        ],
        out_specs=pl.BlockSpec((None, L, inner), lambda bb, c: (bb, c, 0)),
        out_shape=jax.ShapeDtypeStruct((b, s, inner), BF16),
        scratch_shapes=[
            pltpu.VMEM((L + 2 * CONV_PAD, conv_dim), F32),
            pltpu.VMEM((SSM_PAIRS, SSM_STATE, LANES), F32),
            pltpu.VMEM((SSM_PAIRS, L, LANES), F32),
            pltpu.VMEM((SSM_PAIRS, L, LANES), F32),
            pltpu.VMEM((LANES, L), F32),
            pltpu.VMEM((LANES, L), F32),
            pltpu.VMEM((SSM_GROUPS, L, L), F32),
            pltpu.VMEM((SSM_GROUPS, SSM_STATE, L), F32),
            pltpu.VMEM((SSM_GROUPS, L, SSM_STATE), F32),
        ],
        compiler_params=pltpu.CompilerParams(
            dimension_semantics=("parallel", "arbitrary"), vmem_limit_bytes=VMEM_LIMIT),
        name="ssd",
    )(proj3, proj3, dt3, conv_w, conv_b, dt_bias, a_log, d_skip_x, norm_g)


def _mix_kernel(att_ref, yn_ref, ga_ref, gb_ref, x_ref, woa_ref, wos_ref, wout_ref, o_ref):
    ya = jnp.dot(att_ref[...], woa_ref[...], preferred_element_type=F32)
    yb = jnp.dot(yn_ref[...], wos_ref[...], preferred_element_type=F32)
    merged = _sigmoid(ga_ref[...].astype(F32)) * ya + _sigmoid(gb_ref[...].astype(F32)) * yb
    o_ref[...] = x_ref[...] + jnp.dot(merged.astype(BF16), wout_ref[...], preferred_element_type=F32)


def _mix_out(att2d, yn2d, proj2d, x2d, w_oa, w_os, w_out, tm=512):
    m, d = x2d.shape
    ga_col = (proj2d.shape[1] - 2 * d) // d
    row = lambda w, col=0: pl.BlockSpec((tm, w), lambda i, col=col: (i, col))
    full = lambda a: pl.BlockSpec(a.shape, lambda i: (0, 0))
    return pl.pallas_call(
        _mix_kernel,
        grid=(m // tm,),
        in_specs=[row(att2d.shape[1]), row(yn2d.shape[1]), row(d, ga_col), row(d, ga_col + 1), row(d),
                  full(w_oa), full(w_os), full(w_out)],
        out_specs=row(d),
        out_shape=jax.ShapeDtypeStruct((m, d), F32),
        compiler_params=pltpu.CompilerParams(
            dimension_semantics=("parallel",), vmem_limit_bytes=VMEM_LIMIT),
        name="mix_out",
    )(att2d, yn2d, proj2d, proj2d, x2d, w_oa, w_os, w_out)


def _rms(x, g):
    return x * lax.rsqrt(jnp.mean(x * x, axis=-1, keepdims=True) + EPS) * g


def _ffn_kernel(x_ref, g2_ref, wg_ref, wu_ref, wd_ref, p_ref, g3_ref, wpg_ref, wpp_ref, o_ref,
                h_ref, acc_ref):
    j = pl.program_id(1)

    @pl.when(j == 0)
    def _():
        h_ref[...] = _rms(x_ref[...], g2_ref[...]).astype(BF16)
        acc_ref[...] = jnp.zeros_like(acc_ref)

    h = h_ref[...]
    gate = jnp.dot(h, wg_ref[...], preferred_element_type=F32)
    up = jnp.dot(h, wu_ref[...], preferred_element_type=F32)
    act = (gate * _sigmoid(gate) * up).astype(BF16)
    acc_ref[...] += jnp.dot(act, wd_ref[...], preferred_element_type=F32)

    @pl.when(j == pl.num_programs(1) - 1)
    def _():
        x2 = x_ref[...] + acc_ref[...]
        h3 = _rms(x2, g3_ref[...]).astype(BF16)
        pgate = _sigmoid(jnp.dot(h3, wpg_ref[...], preferred_element_type=F32))
        proj = jnp.dot(p_ref[...].astype(BF16), wpp_ref[...], preferred_element_type=F32)
        o_ref[...] = x2 + proj * pgate


def _ffn_ple(x2d, g2, w_gu, w_down, p2d, g3, w_pg, w_pp, tm=512, n_ff=2):
    m, d = x2d.shape
    d_ff = w_down.shape[0]
    tf = d_ff // n_ff
    const = lambda a: pl.BlockSpec(a.shape, lambda i, j: (0, 0))
    return pl.pallas_call(
        _ffn_kernel,
        grid=(m // tm, n_ff),
        in_specs=[
            pl.BlockSpec((tm, d), lambda i, j: (i, 0)),
            const(g2),
            pl.BlockSpec((d, tf), lambda i, j: (0, j)),
            pl.BlockSpec((d, tf), lambda i, j: (0, n_ff + j)),
            pl.BlockSpec((tf, d), lambda i, j: (j, 0)),
            pl.BlockSpec((tm, p2d.shape[1]), lambda i, j: (i, 0)),
            const(g3), const(w_pg), const(w_pp),
        ],
        out_specs=pl.BlockSpec((tm, d), lambda i, j: (i, 0)),
        out_shape=jax.ShapeDtypeStruct((m, d), F32),
        scratch_shapes=[pltpu.VMEM((tm, d), BF16), pltpu.VMEM((tm, d), F32)],
        compiler_params=pltpu.CompilerParams(
            dimension_semantics=("parallel", "arbitrary"), vmem_limit_bytes=VMEM_LIMIT),
        name="ffn_ple",
    )(x2d, g2, w_gu, w_gu, w_down, p2d, g3, w_pg, w_pp)


def _pad_lanes(v):
    return jnp.pad(v.astype(F32), (0, LANES - v.shape[0])).reshape(1, LANES)


def kernel(x, p, ln1_g, w_in, q_norm_g, k_norm_g, w_o_attn, conv_w, conv_b, dt_bias, a_log, d_skip,
           ssm_norm_g, w_o_ssm, w_out, ln2_g, w_gate_up, w_down, ln3_g, w_ple_gate, w_ple_proj):
    b, s, d = x.shape
    assert s % MOBA_BLOCK == 0 and s // MOBA_BLOCK <= MAX_BLOCKS
    attn_w = ATTN_HEADS * HEAD_DIM
    inner = SSM_HEADS * HEAD_DIM
    conv_dim = inner + 2 * SSM_GROUPS * SSM_STATE
    off_z = 3 * attn_w
    off_xbc = off_z + inner
    off_dt = off_xbc + conv_dim
    off_g = off_dt + SSM_HEADS
    row = lambda v: v.astype(F32).reshape(1, -1)

    for i in range(w_in.shape[0]):
        wi = w_in[i]
        w_main = jnp.concatenate(
            [wi[:, :off_z], wi[:, off_xbc:off_dt], wi[:, off_z:off_xbc], wi[:, off_g:]], axis=1).astype(BF16)
        w_dt = jnp.pad(wi[:, off_dt:off_g], ((0, 0), (0, LANES - SSM_HEADS))).astype(BF16)

        x2d = x.reshape(b * s, d)
        proj, dt_raw = _in_proj(x2d, row(ln1_g[i]), w_main, w_dt)
        proj3 = proj.reshape(b, s, -1)

        qn, kn, feat, vt5 = _attn_prep(
            proj3, row(jnp.tile(q_norm_g[i], ATTN_HEADS)), row(jnp.tile(k_norm_g[i], ATTN_HEADS)))
        att = _moba_attention(qn, kn, feat, vt5)

        yn = _ssd(proj3, dt_raw.reshape(b, s, LANES), conv_w[i].astype(F32), row(conv_b[i]),
                  _pad_lanes(dt_bias[i]), _pad_lanes(a_log[i]), row(jnp.repeat(d_skip[i], HEAD_DIM)),
                  row(ssm_norm_g[i]))

        x1 = _mix_out(att.reshape(b * s, attn_w), yn.reshape(b * s, inner), proj, x2d,
                      w_o_attn[i].astype(BF16), w_o_ssm[i].astype(BF16), w_out[i].astype(BF16))
        x2 = _ffn_ple(x1, row(ln2_g[i]), w_gate_up[i].astype(BF16), w_down[i].astype(BF16),
                      p[i].reshape(b * s, -1), row(ln3_g[i]), w_ple_gate[i].astype(BF16),
                      w_ple_proj[i].astype(BF16))
        x = x2.reshape(b, s, d)
    return x
```

```python
import math

import jax
import jax.numpy as jnp
from jax import lax
from jax.experimental import pallas as pl
from jax.experimental.pallas import tpu as pltpu

F32 = jnp.float32
BF16 = jnp.bfloat16

EPS = 1e-6
LOG2E = math.log2(math.e)
LANES = 128
HEAD_DIM = 64
ATTN_HEADS = 16
HEAD_PAIRS = ATTN_HEADS // 2
ATTN_PAIRS_PER_STEP = 4
MOBA_BLOCK = 256
MOBA_TOPK = 3
MAX_BLOCKS = 32
SSM_HEADS = 32
SSM_PAIRS = SSM_HEADS // 2
SSM_GROUPS = 4
SSM_STATE = 128
SSM_CONV = 4
SSM_CHUNK = 128
CONV_PAD = 8
NEG = -1e30
SAFE_EXP2_RANGE = 60.0
VMEM_LIMIT = 56 * 1024 * 1024


def _sigmoid(v):
    return 1.0 / (1.0 + jnp.exp(-v))


def _nt_dot(a, b, **kw):
    return lax.dot_general(a, b, (((1,), (1,)), ((), ())), preferred_element_type=F32, **kw)


def _inproj_kernel(x_ref, g_ref, w_ref, wdt_ref, o_ref, dt_ref, h_ref):
    @pl.when(pl.program_id(1) == 0)
    def _():
        x = x_ref[...]
        ms = jnp.mean(x * x, axis=-1, keepdims=True)
        h = (x * lax.rsqrt(ms + EPS) * g_ref[...]).astype(BF16)
        h_ref[...] = h
        dt_ref[...] = jnp.dot(h, wdt_ref[...], preferred_element_type=F32)

    o_ref[...] = jnp.dot(h_ref[...], w_ref[...], preferred_element_type=F32).astype(o_ref.dtype)


def _in_proj(x2d, g, w_main, w_dt, tm=1024, tn=1024):
    m, d = x2d.shape
    n = w_main.shape[1]
    return pl.pallas_call(
        _inproj_kernel,
        grid=(m // tm, n // tn),
        in_specs=[
            pl.BlockSpec((tm, d), lambda i, j: (i, 0)),
            pl.BlockSpec((1, d), lambda i, j: (0, 0)),
            pl.BlockSpec((d, tn), lambda i, j: (0, j)),
            pl.BlockSpec((d, LANES), lambda i, j: (0, 0)),
        ],
        out_specs=[
            pl.BlockSpec((tm, tn), lambda i, j: (i, j)),
            pl.BlockSpec((tm, LANES), lambda i, j: (i, 0)),
        ],
        out_shape=[jax.ShapeDtypeStruct((m, n), BF16), jax.ShapeDtypeStruct((m, LANES), F32)],
        scratch_shapes=[pltpu.VMEM((tm, d), BF16)],
        compiler_params=pltpu.CompilerParams(
            dimension_semantics=("parallel", "arbitrary"), vmem_limit_bytes=VMEM_LIMIT),
        name="in_proj",
    )(x2d, g, w_main, w_dt)


def _head_rms_norm(x, gain, is_a):
    outs = []
    for c in range(x.shape[1] // LANES):
        blk = x[:, c * LANES:(c + 1) * LANES]
        sq = blk * blk
        sa = jnp.sum(jnp.where(is_a, sq, 0.0), axis=-1, keepdims=True)
        sb = jnp.sum(jnp.where(is_a, 0.0, sq), axis=-1, keepdims=True)
        ms = jnp.where(is_a, sa, sb) * (1.0 / HEAD_DIM)
        outs.append(blk * lax.rsqrt(ms + EPS))
    return jnp.concatenate(outs, axis=1) * gain


def _prep_kernel(q_ref, k_ref, v_ref, qg_ref, kg_ref, qn_ref, kn_ref, sel_ref, vt_ref, kmean_ref):
    i = pl.program_id(1)

    @pl.when(i == 0)
    def _():
        kmean_ref[...] = jnp.zeros_like(kmean_ref)

    lane = lax.broadcasted_iota(jnp.int32, (1, LANES), 1)
    is_a = lane < HEAD_DIM
    qn = _head_rms_norm(q_ref[...].astype(F32), qg_ref[...], is_a) * (HEAD_DIM ** -0.5 * LOG2E)
    kn = _head_rms_norm(k_ref[...].astype(F32), kg_ref[...], is_a)
    qn_ref[...] = qn.astype(BF16)
    kn_ref[...] = kn.astype(BF16)

    vt = v_ref[...].astype(F32).T
    vt_ref[...] = vt.reshape(vt_ref.shape).astype(BF16)

    kmean = kmean_ref[...]
    jidx = lax.broadcasted_iota(jnp.int32, (MAX_BLOCKS, MOBA_BLOCK), 0).astype(F32)
    valid = jidx < i.astype(F32)
    for p in range(HEAD_PAIRS):
        qp = qn[:, p * LANES:(p + 1) * LANES]
        kmp = kmean[:, p * LANES:(p + 1) * LANES]
        for hh in range(2):
            head_lanes = is_a if hh == 0 else jnp.logical_not(is_a)
            km = jnp.where(head_lanes, kmp, 0.0)
            gate = _nt_dot(km, qp, precision=lax.Precision.HIGHEST)
            gate = jnp.where(valid, gate, -jnp.inf)
            sel = jnp.zeros_like(gate)
            for _ in range(MOBA_TOPK):
                top = jnp.max(gate, axis=0, keepdims=True)
                first = jnp.min(jnp.where(gate == top, jidx, float(MAX_BLOCKS)), axis=0, keepdims=True)
                pick = jidx == first
                sel = jnp.where(pick, 1.0, sel)
                gate = jnp.where(pick, -jnp.inf, gate)
            sel_ref[2 * p + hh] = jnp.where(valid, sel, 0.0)

    kmean_ref[pl.ds(i, 1), :] = jnp.mean(kn, axis=0, keepdims=True)


def _attn_prep(proj3, qg, kg):
    b, s, _ = proj3.shape
    nb = s // MOBA_BLOCK
    width = ATTN_HEADS * HEAD_DIM
    tile = lambda col: pl.BlockSpec((None, MOBA_BLOCK, width), lambda bb, i, col=col: (bb, i, col))
    gain = pl.BlockSpec((1, width), lambda bb, i: (0, 0))
    return pl.pallas_call(
        _prep_kernel,
        grid=(b, nb),
        in_specs=[tile(0), tile(1), tile(2), gain, gain],
        out_specs=[
            tile(0), tile(0),
            pl.BlockSpec((None, ATTN_HEADS, None, MAX_BLOCKS, MOBA_BLOCK), lambda bb, i: (bb, 0, i, 0, 0)),
            pl.BlockSpec((None, HEAD_PAIRS, None, LANES, MOBA_BLOCK), lambda bb, i: (bb, 0, i, 0, 0)),
        ],
        out_shape=[
            jax.ShapeDtypeStruct((b, s, width), BF16),
            jax.ShapeDtypeStruct((b, s, width), BF16),
            jax.ShapeDtypeStruct((b, ATTN_HEADS, nb, MAX_BLOCKS, MOBA_BLOCK), F32),
            jax.ShapeDtypeStruct((b, HEAD_PAIRS, nb, LANES, MOBA_BLOCK), BF16),
        ],
        scratch_shapes=[pltpu.VMEM((MAX_BLOCKS, width), F32)],
        compiler_params=pltpu.CompilerParams(
            dimension_semantics=("parallel", "arbitrary"), vmem_limit_bytes=VMEM_LIMIT),
        name="attn_prep",
    )(proj3, proj3, proj3, qg, kg)


def _attn_kernel(bounded_ref, q_ref, k_ref, vt_ref, sel_ref, o_ref, qm_ref, acc_ref, s_ref, p_ref):
    i = pl.program_id(2)
    heads = 2 * ATTN_PAIRS_PER_STEP
    lane = lax.broadcasted_iota(jnp.int32, (1, LANES), 1)
    is_a = lane < HEAD_DIM
    zero = jnp.zeros((), BF16)
    for pp in range(ATTN_PAIRS_PER_STEP):
        qp = q_ref[:, pp * LANES:(pp + 1) * LANES]
        qm_ref[2 * pp] = jnp.where(is_a, qp, zero)
        qm_ref[2 * pp + 1] = jnp.where(is_a, zero, qp)

    def scores(j, h):
        pp = h // 2
        kt = k_ref[pl.ds(pl.multiple_of(j * MOBA_BLOCK, MOBA_BLOCK), MOBA_BLOCK), pp * LANES:(pp + 1) * LANES]
        return _nt_dot(kt, qm_ref[h])

    def values(j, h):
        return vt_ref[h // 2, j, (h % 2) * HEAD_DIM:(h % 2 + 1) * HEAD_DIM, :]

    krow = lax.broadcasted_iota(jnp.int32, (MOBA_BLOCK, MOBA_BLOCK), 0)
    qcol = lax.broadcasted_iota(jnp.int32, (MOBA_BLOCK, MOBA_BLOCK), 1)
    causal = krow <= qcol
    m0, l0 = [], []
    for h in range(heads):
        s = jnp.where(causal, scores(i, h), NEG)
        m0.append(jnp.max(s, axis=0, keepdims=True))
        s_ref[h] = s
    for h in range(heads):
        p = jnp.exp2(s_ref[h] - m0[h])
        l0.append(jnp.sum(p, axis=0, keepdims=True))
        p_ref[0, h] = p.astype(BF16)
    for h in range(heads):
        acc_ref[h] = jnp.dot(values(i, h), p_ref[0, h], preferred_element_type=F32)

    def fixed_ref_loop(ls):
        def body(jj, ls):
            ls = list(ls)
            for u in range(2):
                j = 2 * jj + u
                for h in range(heads):
                    picked = sel_ref[h, pl.ds(j, 1), :] > 0.0
                    if u == 1:
                        picked = jnp.logical_and(picked, j < i)
                    p = jnp.exp2(scores(j, h) - jnp.where(picked, m0[h], -NEG))
                    ls[h] = ls[h] + jnp.sum(p, axis=0, keepdims=True)
                    p_ref[u, h] = p.astype(BF16)
            for u in range(2):
                for h in range(heads):
                    acc_ref[h] += jnp.dot(values(2 * jj + u, h), p_ref[u, h], preferred_element_type=F32)
            return tuple(ls)

        return lax.fori_loop(0, (i + 1) // 2, body, ls)

    def running_max_loop(ls):
        def body(j, carry):
            ms, ls = carry
            tile_max, new_m, new_l, alphas = [], [], [], []
            for h in range(heads):
                s = scores(j, h)
                tile_max.append(jnp.max(s, axis=0, keepdims=True))
                s_ref[h] = s
            for h in range(heads):
                picked = sel_ref[h, pl.ds(j, 1), :] > 0.0
                m = jnp.maximum(ms[h], jnp.where(picked, tile_max[h], NEG))
                p = jnp.exp2(s_ref[h] - jnp.where(picked, m, -NEG))
                alpha = jnp.exp2(ms[h] - m)
                new_m.append(m)
                new_l.append(alpha * ls[h] + jnp.sum(p, axis=0, keepdims=True))
                alphas.append(alpha)
                p_ref[0, h] = p.astype(BF16)
            for h in range(heads):
                acc_ref[h] = alphas[h] * acc_ref[h] + jnp.dot(values(j, h), p_ref[0, h],
                                                              preferred_element_type=F32)
            return tuple(new_m), tuple(new_l)

        return lax.fori_loop(0, i, body, (tuple(m0), ls))[1]

    ls = lax.cond(bounded_ref[0] != 0, fixed_ref_loop, running_max_loop, tuple(l0))
    for pp in range(ATTN_PAIRS_PER_STEP):
        out_t = jnp.concatenate([acc_ref[2 * pp] * (1.0 / ls[2 * pp]),
                                 acc_ref[2 * pp + 1] * (1.0 / ls[2 * pp + 1])], axis=0)
        o_ref[:, pp * LANES:(pp + 1) * LANES] = out_t.T.astype(o_ref.dtype)


def _moba_attention(bounded, qn, kn, sel, vt5):
    b, s, width = qn.shape
    nb = s // MOBA_BLOCK
    pps = ATTN_PAIRS_PER_STEP
    gw = pps * LANES
    qtile = pl.BlockSpec((None, MOBA_BLOCK, gw), lambda bb, g, i: (bb, i, g))
    return pl.pallas_call(
        _attn_kernel,
        grid=(b, HEAD_PAIRS // pps, nb),
        in_specs=[
            pl.BlockSpec(memory_space=pltpu.SMEM),
            qtile,
            pl.BlockSpec((None, s, gw), lambda bb, g, i: (bb, 0, g)),
            pl.BlockSpec((None, pps, nb, LANES, MOBA_BLOCK), lambda bb, g, i: (bb, g, 0, 0, 0)),
            pl.BlockSpec((None, 2 * pps, None, MAX_BLOCKS, MOBA_BLOCK), lambda bb, g, i: (bb, g, i, 0, 0)),
        ],
        out_specs=qtile,
        out_shape=jax.ShapeDtypeStruct((b, s, width), BF16),
        scratch_shapes=[
            pltpu.VMEM((2 * pps, MOBA_BLOCK, LANES), BF16),
            pltpu.VMEM((2 * pps, HEAD_DIM, MOBA_BLOCK), F32),
            pltpu.VMEM((2 * pps, MOBA_BLOCK, MOBA_BLOCK), F32),
            pltpu.VMEM((2, 2 * pps, MOBA_BLOCK, MOBA_BLOCK), BF16),
        ],
        compiler_params=pltpu.CompilerParams(
            dimension_semantics=("parallel", "parallel", "arbitrary"), vmem_limit_bytes=VMEM_LIMIT),
        name="moba_attn",
    )(bounded, qn, kn, vt5, sel)


def _ssd_kernel(xbc_ref, z_ref, dt_ref, cw_ref, cb_ref, dtb_ref, alog_ref, dsk_ref, ng_ref, o_ref,
                ext_ref, act_ref, state_ref, y_ref, acst_ref, dtt_ref, cbm_ref, bt_ref):
    c = pl.program_id(1)
    L = SSM_CHUNK
    n_blk = act_ref.shape[0]
    b_blk = SSM_PAIRS
    c_blk = SSM_PAIRS + SSM_GROUPS

    @pl.when(c == 0)
    def _():
        ext_ref[:, 0:CONV_PAD, :] = jnp.zeros((n_blk, CONV_PAD, LANES), F32)
        state_ref[...] = jnp.zeros_like(state_ref)

    for blk in range(n_blk):
        ext_ref[blk, CONV_PAD:CONV_PAD + L, :] = xbc_ref[:, blk * LANES:(blk + 1) * LANES].astype(F32)

    def conv_body(blk, carry):
        w = cw_ref[blk]
        conv = cb_ref[blk]
        for k in range(SSM_CONV):
            start = CONV_PAD - (SSM_CONV - 1) + k
            conv = conv + w[k:k + 1, :] * ext_ref[blk, start:start + L, :]
        ext_ref[blk, 0:CONV_PAD, :] = ext_ref[blk, L:L + CONV_PAD, :]
        act_ref[blk] = conv * _sigmoid(conv)
        return carry

    lax.fori_loop(0, n_blk, conv_body, 0, unroll=2)

    for g in range(SSM_GROUPS):
        bg = act_ref[b_blk + g]
        cbm_ref[g] = _nt_dot(act_ref[c_blk + g].astype(BF16), bg.astype(BF16))
        bt_ref[g] = bg.T

    dt_in = dt_ref[...] + dtb_ref[...]
    dtv = jnp.maximum(dt_in, 0.0) + jnp.log(1.0 + jnp.exp(-jnp.abs(dt_in)))
    a2 = dtv * (-LOG2E * jnp.exp(alog_ref[...]))
    row = lax.broadcasted_iota(jnp.int32, (L, L), 0)
    col = lax.broadcasted_iota(jnp.int32, (L, L), 1)
    tri = col <= row
    acs = jnp.dot(tri.astype(F32), a2, preferred_element_type=F32, precision=lax.Precision.HIGHEST)
    acst_ref[...] = acs.T
    dtt_ref[...] = dtv.T
    lane = lax.broadcasted_iota(jnp.int32, (1, LANES), 1)

    def pair_body(j, carry):
        g = j // (SSM_PAIRS // SSM_GROUPS)
        cbm = cbm_ref[g]
        bt = bt_ref[g]
        cg = act_ref[c_blk + g]
        xp = act_ref[j].astype(BF16)
        prev = state_ref[j]
        rhs = jnp.concatenate([xp, prev.astype(BF16)], axis=0)
        y_pair = jnp.zeros((L, LANES), F32)
        s_new = jnp.zeros((SSM_STATE, LANES), F32)
        s_dec = jnp.zeros((1, LANES), F32)
        for hh in range(2):
            h = 2 * j + hh
            arow = acst_ref[pl.ds(h, 1), :]
            drow = dtt_ref[pl.ds(h, 1), :]
            rmat = jnp.broadcast_to(arow, (L, L))
            cmat = rmat.T
            decay = jnp.exp2(jnp.where(tri, cmat - rmat, NEG))
            m_in = (decay * cbm * drow).astype(BF16)
            c_in = (cg * jnp.exp2(cmat)).astype(BF16)
            y_full = jnp.dot(jnp.concatenate([m_in, c_in], axis=1), rhs, preferred_element_type=F32)
            tot = arow[:, L - 1:L]
            srow = drow * jnp.exp2(tot - arow)
            s_full = jnp.dot((bt * srow).astype(BF16), xp, preferred_element_type=F32)
            half = (lane >= HEAD_DIM) if hh == 1 else (lane < HEAD_DIM)
            y_pair = jnp.where(half, y_full, y_pair)
            s_new = jnp.where(half, s_full, s_new)
            s_dec = jnp.where(half, jnp.exp2(tot), s_dec)
        y_ref[j] = y_pair
        state_ref[j] = prev * s_dec + s_new
        return carry

    lax.fori_loop(0, SSM_PAIRS, pair_body, 0, unroll=2)

    ppg = SSM_PAIRS // SSM_GROUPS
    for g in range(SSM_GROUPS):
        ys = []
        ssq = jnp.zeros((L, 1), F32)
        for j in range(g * ppg, (g + 1) * ppg):
            cols = slice(j * LANES, (j + 1) * LANES)
            zf = z_ref[:, cols].astype(F32)
            y = (y_ref[j] + dsk_ref[:, cols] * act_ref[j]) * (zf * _sigmoid(zf))
            ssq = ssq + jnp.sum(y * y, axis=-1, keepdims=True)
            ys.append(y)
        scale = lax.rsqrt(ssq * (1.0 / (ppg * LANES)) + EPS)
        for j, y in zip(range(g * ppg, (g + 1) * ppg), ys):
            cols = slice(j * LANES, (j + 1) * LANES)
            o_ref[:, cols] = (y * scale * ng_ref[:, cols]).astype(o_ref.dtype)


def _ssd(proj3, dt3, conv_w, conv_b, dt_bias, a_log, d_skip_x, norm_g):
    b, s, _ = proj3.shape
    L = SSM_CHUNK
    inner = SSM_HEADS * HEAD_DIM
    conv_dim = inner + 2 * SSM_GROUPS * SSM_STATE
    n_blk = conv_dim // LANES
    vec = lambda w: pl.BlockSpec((1, w), lambda bb, c: (0, 0))
    return pl.pallas_call(
        _ssd_kernel,
        grid=(b, s // L),
        in_specs=[
            pl.BlockSpec((None, L, conv_dim), lambda bb, c: (bb, c, 1)),
            pl.BlockSpec((None, L, inner), lambda bb, c: (bb, c, 3)),
            pl.BlockSpec((None, L, LANES), lambda bb, c: (bb, c, 0)),
            pl.BlockSpec((n_blk, SSM_CONV, LANES), lambda bb, c: (0, 0, 0)),
            pl.BlockSpec((n_blk, 1, LANES), lambda bb, c: (0, 0, 0)),
            vec(LANES), vec(LANES), vec(inner), vec(inner),
        ],
        out_specs=pl.BlockSpec((None, L, inner), lambda bb, c: (bb, c, 0)),
        out_shape=jax.ShapeDtypeStruct((b, s, inner), BF16),
        scratch_shapes=[
            pltpu.VMEM((n_blk, L + CONV_PAD, LANES), F32),
            pltpu.VMEM((n_blk, L, LANES), F32),
            pltpu.VMEM((SSM_PAIRS, SSM_STATE, LANES), F32),
            pltpu.VMEM((SSM_PAIRS, L, LANES), F32),
            pltpu.VMEM((LANES, L), F32),
            pltpu.VMEM((LANES, L), F32),
            pltpu.VMEM((SSM_GROUPS, L, L), F32),
            pltpu.VMEM((SSM_GROUPS, SSM_STATE, L), F32),
        ],
        compiler_params=pltpu.CompilerParams(
            dimension_semantics=("parallel", "arbitrary"), vmem_limit_bytes=VMEM_LIMIT),
        name="ssd",
    )(proj3, proj3, dt3, conv_w, conv_b, dt_bias, a_log, d_skip_x, norm_g)


def _mix_kernel(att_ref, yn_ref, ga_ref, gb_ref, x_ref, woa_ref, wos_ref, wout_ref, o_ref):
    ya = jnp.dot(att_ref[...], woa_ref[...], preferred_element_type=F32)
    yb = jnp.dot(yn_ref[...], wos_ref[...], preferred_element_type=F32)
    merged = _sigmoid(ga_ref[...].astype(F32)) * ya + _sigmoid(gb_ref[...].astype(F32)) * yb
    o_ref[...] = x_ref[...] + jnp.dot(merged.astype(BF16), wout_ref[...], preferred_element_type=F32)


def _mix_out(att2d, yn2d, proj2d, x2d, w_oa, w_os, w_out, tm=512):
    m, d = x2d.shape
    ga_col = (proj2d.shape[1] - 2 * d) // d
    row = lambda w, col=0: pl.BlockSpec((tm, w), lambda i, col=col: (i, col))
    full = lambda a: pl.BlockSpec(a.shape, lambda i: (0, 0))
    return pl.pallas_call(
        _mix_kernel,
        grid=(m // tm,),
        in_specs=[row(att2d.shape[1]), row(yn2d.shape[1]), row(d, ga_col), row(d, ga_col + 1), row(d),
                  full(w_oa), full(w_os), full(w_out)],
        out_specs=row(d),
        out_shape=jax.ShapeDtypeStruct((m, d), F32),
        compiler_params=pltpu.CompilerParams(
            dimension_semantics=("parallel",), vmem_limit_bytes=VMEM_LIMIT),
        name="mix_out",
    )(att2d, yn2d, proj2d, proj2d, x2d, w_oa, w_os, w_out)


def _rms(x, g):
    return x * lax.rsqrt(jnp.mean(x * x, axis=-1, keepdims=True) + EPS) * g


def _ffn_kernel(x_ref, g2_ref, wg_ref, wu_ref, wd_ref, p_ref, g3_ref, wpg_ref, wpp_ref, o_ref,
                h_ref, acc_ref):
    j = pl.program_id(1)

    @pl.when(j == 0)
    def _():
        h_ref[...] = _rms(x_ref[...], g2_ref[...]).astype(BF16)
        acc_ref[...] = jnp.zeros_like(acc_ref)

    h = h_ref[...]
    gate = jnp.dot(h, wg_ref[...], preferred_element_type=F32)
    up = jnp.dot(h, wu_ref[...], preferred_element_type=F32)
    act = (gate * _sigmoid(gate) * up).astype(BF16)
    acc_ref[...] += jnp.dot(act, wd_ref[...], preferred_element_type=F32)

    @pl.when(j == pl.num_programs(1) - 1)
    def _():
        x2 = x_ref[...] + acc_ref[...]
        h3 = _rms(x2, g3_ref[...]).astype(BF16)
        pgate = _sigmoid(jnp.dot(h3, wpg_ref[...], preferred_element_type=F32))
        proj = jnp.dot(p_ref[...].astype(BF16), wpp_ref[...], preferred_element_type=F32)
        o_ref[...] = x2 + proj * pgate


def _ffn_ple(x2d, g2, w_gu, w_down, p2d, g3, w_pg, w_pp, tm=512, n_ff=2):
    m, d = x2d.shape
    d_ff = w_down.shape[0]
    tf = d_ff // n_ff
    const = lambda a: pl.BlockSpec(a.shape, lambda i, j: (0, 0))
    return pl.pallas_call(
        _ffn_kernel,
        grid=(m // tm, n_ff),
        in_specs=[
            pl.BlockSpec((tm, d), lambda i, j: (i, 0)),
            const(g2),
            pl.BlockSpec((d, tf), lambda i, j: (0, j)),
            pl.BlockSpec((d, tf), lambda i, j: (0, n_ff + j)),
            pl.BlockSpec((tf, d), lambda i, j: (j, 0)),
            pl.BlockSpec((tm, p2d.shape[1]), lambda i, j: (i, 0)),
            const(g3), const(w_pg), const(w_pp),
        ],
        out_specs=pl.BlockSpec((tm, d), lambda i, j: (i, 0)),
        out_shape=jax.ShapeDtypeStruct((m, d), F32),
        scratch_shapes=[pltpu.VMEM((tm, d), BF16), pltpu.VMEM((tm, d), F32)],
        compiler_params=pltpu.CompilerParams(
            dimension_semantics=("parallel", "arbitrary"), vmem_limit_bytes=VMEM_LIMIT),
        name="ffn_ple",
    )(x2d, g2, w_gu, w_gu, w_down, p2d, g3, w_pg, w_pp)


def _pad_lanes(v):
    return jnp.pad(v.astype(F32), (0, LANES - v.shape[0])).reshape(1, LANES)


def _lane_blocks(a):
    r, w = a.shape
    return a.astype(F32).reshape(r, w // LANES, LANES).transpose(1, 0, 2)


def kernel(x, p, ln1_g, w_in, q_norm_g, k_norm_g, w_o_attn, conv_w, conv_b, dt_bias, a_log, d_skip,
           ssm_norm_g, w_o_ssm, w_out, ln2_g, w_gate_up, w_down, ln3_g, w_ple_gate, w_ple_proj):
    b, s, d = x.shape
    assert s % MOBA_BLOCK == 0 and s // MOBA_BLOCK <= MAX_BLOCKS
    attn_w = ATTN_HEADS * HEAD_DIM
    inner = SSM_HEADS * HEAD_DIM
    conv_dim = inner + 2 * SSM_GROUPS * SSM_STATE
    off_z = 3 * attn_w
    off_xbc = off_z + inner
    off_dt = off_xbc + conv_dim
    off_g = off_dt + SSM_HEADS
    row = lambda v: v.astype(F32).reshape(1, -1)

    for i in range(w_in.shape[0]):
        wi = w_in[i]
        w_main = jnp.concatenate(
            [wi[:, :off_z], wi[:, off_xbc:off_dt], wi[:, off_z:off_xbc], wi[:, off_g:]], axis=1).astype(BF16)
        w_dt = jnp.pad(wi[:, off_dt:off_g], ((0, 0), (0, LANES - SSM_HEADS))).astype(BF16)

        x2d = x.reshape(b * s, d)
        proj, dt_raw = _in_proj(x2d, row(ln1_g[i]), w_main, w_dt)
        proj3 = proj.reshape(b, s, -1)

        qn, kn, sel, vt5 = _attn_prep(
            proj3, row(jnp.tile(q_norm_g[i], ATTN_HEADS)), row(jnp.tile(k_norm_g[i], ATTN_HEADS)))
        score_bound = 1.02 * LOG2E * HEAD_DIM ** 0.5 * jnp.max(jnp.abs(q_norm_g[i])) * jnp.max(jnp.abs(k_norm_g[i]))
        bounded = (2.0 * score_bound <= SAFE_EXP2_RANGE).astype(jnp.int32).reshape(1)
        att = _moba_attention(bounded, qn, kn, sel, vt5)

        yn = _ssd(proj3, dt_raw.reshape(b, s, LANES), _lane_blocks(conv_w[i]), _lane_blocks(row(conv_b[i])),
                  _pad_lanes(dt_bias[i]), _pad_lanes(a_log[i]), row(jnp.repeat(d_skip[i], HEAD_DIM)),
                  row(ssm_norm_g[i]))

        x1 = _mix_out(att.reshape(b * s, attn_w), yn.reshape(b * s, inner), proj, x2d,
                      w_o_attn[i].astype(BF16), w_o_ssm[i].astype(BF16), w_out[i].astype(BF16))
        x2 = _ffn_ple(x1, row(ln2_g[i]), w_gate_up[i].astype(BF16), w_down[i].astype(BF16),
                      p[i].reshape(b * s, -1), row(ln3_g[i]), w_ple_gate[i].astype(BF16),
                      w_ple_proj[i].astype(BF16))
        x = x2.reshape(b, s, d)
    return x
```

```python
import functools
import math

import jax
import jax.numpy as jnp
from jax import lax
from jax.experimental import pallas as pl
from jax.experimental.pallas import tpu as pltpu

F32 = jnp.float32
BF16 = jnp.bfloat16

EPS = 1e-6
LOG2E = math.log2(math.e)
LANES = 128
HEAD_DIM = 64
ATTN_HEADS = 16
HEAD_PAIRS = ATTN_HEADS // 2
ATTN_PAIRS_PER_STEP = 4
MOBA_BLOCK = 256
MOBA_TOPK = 3
MAX_BLOCKS = 32
SSM_HEADS = 32
SSM_PAIRS = SSM_HEADS // 2
SSM_GROUPS = 4
SSM_STATE = 128
SSM_CONV = 4
SSM_CHUNK = 128
CONV_PAD = 8
NEG = -1e30
SAFE_EXP2_RANGE = 60.0
VMEM_LIMIT = 56 * 1024 * 1024


def _sigmoid(v):
    return 1.0 / (1.0 + jnp.exp(-v))


def _nt_dot(a, b, **kw):
    return lax.dot_general(a, b, (((1,), (1,)), ((), ())), preferred_element_type=F32, **kw)


def _inproj_kernel(starts, x_ref, g_ref, wdt_ref, *rest):
    w_refs, (o_ref, dt_ref, h_ref) = rest[:len(starts)], rest[len(starts):]
    j = pl.program_id(1)

    @pl.when(j == 0)
    def _():
        x = x_ref[...]
        ms = jnp.mean(x * x, axis=-1, keepdims=True)
        h = (x * lax.rsqrt(ms + EPS) * g_ref[...]).astype(BF16)
        h_ref[...] = h
        dt_ref[...] = jnp.dot(h, wdt_ref[...], preferred_element_type=F32)

    bounds = list(starts) + [pl.num_programs(1)]
    for seg, w_ref in enumerate(w_refs):
        @pl.when(jnp.logical_and(j >= bounds[seg], j < bounds[seg + 1]))
        def _(w_ref=w_ref):
            o_ref[...] = jnp.dot(h_ref[...], w_ref[...], preferred_element_type=F32).astype(o_ref.dtype)


def _in_proj(x2d, g, w_segs, w_dt, tm=1024, tn=1024):
    m, d = x2d.shape
    tiles = [w.shape[1] // tn for w in w_segs]
    starts = [sum(tiles[:k]) for k in range(len(tiles))]
    n = tn * sum(tiles)

    def seg_spec(start, count):
        return pl.BlockSpec((d, tn), lambda i, j: (0, jnp.clip(j - start, 0, count - 1)))

    return pl.pallas_call(
        functools.partial(_inproj_kernel, tuple(starts)),
        grid=(m // tm, n // tn),
        in_specs=[
            pl.BlockSpec((tm, d), lambda i, j: (i, 0)),
            pl.BlockSpec((1, d), lambda i, j: (0, 0)),
            pl.BlockSpec((d, LANES), lambda i, j: (0, 0)),
        ] + [seg_spec(st, ct) for st, ct in zip(starts, tiles)],
        out_specs=[
            pl.BlockSpec((tm, tn), lambda i, j: (i, j)),
            pl.BlockSpec((tm, LANES), lambda i, j: (i, 0)),
        ],
        out_shape=[jax.ShapeDtypeStruct((m, n), BF16), jax.ShapeDtypeStruct((m, LANES), F32)],
        scratch_shapes=[pltpu.VMEM((tm, d), BF16)],
        compiler_params=pltpu.CompilerParams(
            dimension_semantics=("parallel", "arbitrary"), vmem_limit_bytes=VMEM_LIMIT),
        name="in_proj",
    )(x2d, g, w_dt, *w_segs)


def _head_rms_norm(x, gain, is_a):
    outs = []
    for c in range(x.shape[1] // LANES):
        blk = x[:, c * LANES:(c + 1) * LANES]
        sq = blk * blk
        sa = jnp.sum(jnp.where(is_a, sq, 0.0), axis=-1, keepdims=True)
        sb = jnp.sum(jnp.where(is_a, 0.0, sq), axis=-1, keepdims=True)
        ms = jnp.where(is_a, sa, sb) * (1.0 / HEAD_DIM)
        outs.append(blk * lax.rsqrt(ms + EPS))
    return jnp.concatenate(outs, axis=1) * gain


def _prep_kernel(q_ref, k_ref, v_ref, qg_ref, kg_ref, qn_ref, kn_ref, sel_ref, vt_ref, kmean_ref):
    i = pl.program_id(1)

    @pl.when(i == 0)
    def _():
        kmean_ref[...] = jnp.zeros_like(kmean_ref)

    lane = lax.broadcasted_iota(jnp.int32, (1, LANES), 1)
    is_a = lane < HEAD_DIM
    qn = _head_rms_norm(q_ref[...].astype(F32), qg_ref[...], is_a) * (HEAD_DIM ** -0.5 * LOG2E)
    kn = _head_rms_norm(k_ref[...].astype(F32), kg_ref[...], is_a)
    kn_ref[...] = kn.astype(BF16)

    vt = v_ref[...].astype(F32).T
    vt_ref[...] = vt.reshape(vt_ref.shape).astype(BF16)

    kmean = kmean_ref[...]
    q_hi = qn.astype(BF16)
    qn_ref[...] = q_hi
    q_lo = (qn - q_hi.astype(F32)).astype(BF16)
    km_hi = kmean.astype(BF16)
    km_lo = (kmean - km_hi.astype(F32)).astype(BF16)
    zero = jnp.zeros((), BF16)
    jidx = lax.broadcasted_iota(jnp.int32, (MAX_BLOCKS, MOBA_BLOCK), 0).astype(F32)
    valid = jidx < i.astype(F32)
    for p in range(HEAD_PAIRS):
        cols = slice(p * LANES, (p + 1) * LANES)
        km4 = jnp.concatenate([jnp.where(is_a, km_hi[:, cols], zero), jnp.where(is_a, zero, km_hi[:, cols]),
                               jnp.where(is_a, km_lo[:, cols], zero), jnp.where(is_a, zero, km_lo[:, cols])],
                              axis=0)
        g_hi = _nt_dot(km4, q_hi[:, cols])
        g_lo = _nt_dot(km4[:2 * MAX_BLOCKS], q_lo[:, cols])
        for hh in range(2):
            r0 = hh * MAX_BLOCKS
            gate = (g_hi[r0:r0 + MAX_BLOCKS] + g_hi[r0 + 2 * MAX_BLOCKS:r0 + 3 * MAX_BLOCKS]
                    + g_lo[r0:r0 + MAX_BLOCKS])
            gate = jnp.where(valid, gate, -jnp.inf)
            sel = jnp.zeros_like(gate)
            for _ in range(MOBA_TOPK):
                top = jnp.max(gate, axis=0, keepdims=True)
                first = jnp.min(jnp.where(gate == top, jidx, float(MAX_BLOCKS)), axis=0, keepdims=True)
                pick = jidx == first
                sel = jnp.where(pick, 1.0, sel)
                gate = jnp.where(pick, -jnp.inf, gate)
            sel_ref[2 * p + hh] = jnp.where(valid, sel, 0.0)

    kmean_ref[pl.ds(i, 1), :] = jnp.mean(kn, axis=0, keepdims=True)


def _attn_prep(proj3, qg, kg):
    b, s, _ = proj3.shape
    nb = s // MOBA_BLOCK
    width = ATTN_HEADS * HEAD_DIM
    tile = lambda col: pl.BlockSpec((None, MOBA_BLOCK, width), lambda bb, i, col=col: (bb, i, col))
    gain = pl.BlockSpec((1, width), lambda bb, i: (0, 0))
    return pl.pallas_call(
        _prep_kernel,
        grid=(b, nb),
        in_specs=[tile(0), tile(1), tile(2), gain, gain],
        out_specs=[
            tile(0), tile(0),
            pl.BlockSpec((None, ATTN_HEADS, None, MAX_BLOCKS, MOBA_BLOCK), lambda bb, i: (bb, 0, i, 0, 0)),
            pl.BlockSpec((None, HEAD_PAIRS, None, LANES, MOBA_BLOCK), lambda bb, i: (bb, 0, i, 0, 0)),
        ],
        out_shape=[
            jax.ShapeDtypeStruct((b, s, width), BF16),
            jax.ShapeDtypeStruct((b, s, width), BF16),
            jax.ShapeDtypeStruct((b, ATTN_HEADS, nb, MAX_BLOCKS, MOBA_BLOCK), F32),
            jax.ShapeDtypeStruct((b, HEAD_PAIRS, nb, LANES, MOBA_BLOCK), BF16),
        ],
        scratch_shapes=[pltpu.VMEM((MAX_BLOCKS, width), F32)],
        compiler_params=pltpu.CompilerParams(
            dimension_semantics=("parallel", "arbitrary"), vmem_limit_bytes=VMEM_LIMIT),
        name="attn_prep",
    )(proj3, proj3, proj3, qg, kg)


def _attn_kernel(bounded_ref, q_ref, k_ref, vt_ref, sel_ref, o_ref, qm_ref, acc_ref, s_ref, p_ref):
    i = pl.program_id(2)
    heads = 2 * ATTN_PAIRS_PER_STEP
    lane = lax.broadcasted_iota(jnp.int32, (1, LANES), 1)
    is_a = lane < HEAD_DIM
    zero = jnp.zeros((), BF16)
    for pp in range(ATTN_PAIRS_PER_STEP):
        qp = q_ref[:, pp * LANES:(pp + 1) * LANES]
        qm_ref[2 * pp] = jnp.where(is_a, qp, zero)
        qm_ref[2 * pp + 1] = jnp.where(is_a, zero, qp)

    def scores(j, h):
        pp = h // 2
        kt = k_ref[pl.ds(pl.multiple_of(j * MOBA_BLOCK, MOBA_BLOCK), MOBA_BLOCK), pp * LANES:(pp + 1) * LANES]
        return _nt_dot(kt, qm_ref[h])

    def values(j, h):
        return vt_ref[h // 2, j, (h % 2) * HEAD_DIM:(h % 2 + 1) * HEAD_DIM, :]

    krow = lax.broadcasted_iota(jnp.int32, (MOBA_BLOCK, MOBA_BLOCK), 0)
    qcol = lax.broadcasted_iota(jnp.int32, (MOBA_BLOCK, MOBA_BLOCK), 1)
    causal = krow <= qcol
    m0, l0 = [], []
    for h in range(heads):
        s = jnp.where(causal, scores(i, h), NEG)
        m0.append(jnp.max(s, axis=0, keepdims=True))
        s_ref[h] = s
    for h in range(heads):
        p = jnp.exp2(s_ref[h] - m0[h])
        l0.append(jnp.sum(p, axis=0, keepdims=True))
        p_ref[0, h] = p.astype(BF16)
    for h in range(heads):
        acc_ref[h] = jnp.dot(values(i, h), p_ref[0, h], preferred_element_type=F32)

    def fixed_ref_loop(ls):
        def body(jj, ls):
            ls = list(ls)
            for u in range(2):
                j = 2 * jj + u
                for h in range(heads):
                    picked = sel_ref[h, pl.ds(j, 1), :] > 0.0
                    if u == 1:
                        picked = jnp.logical_and(picked, j < i)
                    p = jnp.exp2(scores(j, h) - jnp.where(picked, m0[h], -NEG))
                    ls[h] = ls[h] + jnp.sum(p, axis=0, keepdims=True)
                    p_ref[u, h] = p.astype(BF16)
            for u in range(2):
                for h in range(heads):
                    acc_ref[h] += jnp.dot(values(2 * jj + u, h), p_ref[u, h], preferred_element_type=F32)
            return tuple(ls)

        return lax.fori_loop(0, (i + 1) // 2, body, ls)

    def running_max_loop(ls):
        def body(j, carry):
            ms, ls = carry
            tile_max, new_m, new_l, alphas = [], [], [], []
            for h in range(heads):
                s = scores(j, h)
                tile_max.append(jnp.max(s, axis=0, keepdims=True))
                s_ref[h] = s
            for h in range(heads):
                picked = sel_ref[h, pl.ds(j, 1), :] > 0.0
                m = jnp.maximum(ms[h], jnp.where(picked, tile_max[h], NEG))
                p = jnp.exp2(s_ref[h] - jnp.where(picked, m, -NEG))
                alpha = jnp.exp2(ms[h] - m)
                new_m.append(m)
                new_l.append(alpha * ls[h] + jnp.sum(p, axis=0, keepdims=True))
                alphas.append(alpha)
                p_ref[0, h] = p.astype(BF16)
            for h in range(heads):
                acc_ref[h] = alphas[h] * acc_ref[h] + jnp.dot(values(j, h), p_ref[0, h],
                                                              preferred_element_type=F32)
            return tuple(new_m), tuple(new_l)

        return lax.fori_loop(0, i, body, (tuple(m0), ls))[1]

    ls = lax.cond(bounded_ref[0] != 0, fixed_ref_loop, running_max_loop, tuple(l0))
    for pp in range(ATTN_PAIRS_PER_STEP):
        out_t = jnp.concatenate([acc_ref[2 * pp] * (1.0 / ls[2 * pp]),
                                 acc_ref[2 * pp + 1] * (1.0 / ls[2 * pp + 1])], axis=0)
        o_ref[:, pp * LANES:(pp + 1) * LANES] = out_t.T.astype(o_ref.dtype)


def _moba_attention(bounded, qn, kn, sel, vt5):
    b, s, width = qn.shape
    nb = s // MOBA_BLOCK
    pps = ATTN_PAIRS_PER_STEP
    gw = pps * LANES
    qtile = pl.BlockSpec((None, MOBA_BLOCK, gw), lambda bb, g, i: (bb, i, g))
    return pl.pallas_call(
        _attn_kernel,
        grid=(b, HEAD_PAIRS // pps, nb),
        in_specs=[
            pl.BlockSpec(memory_space=pltpu.SMEM),
            qtile,
            pl.BlockSpec((None, s, gw), lambda bb, g, i: (bb, 0, g)),
            pl.BlockSpec((None, pps, nb, LANES, MOBA_BLOCK), lambda bb, g, i: (bb, g, 0, 0, 0)),
            pl.BlockSpec((None, 2 * pps, None, MAX_BLOCKS, MOBA_BLOCK), lambda bb, g, i: (bb, g, i, 0, 0)),
        ],
        out_specs=qtile,
        out_shape=jax.ShapeDtypeStruct((b, s, width), BF16),
        scratch_shapes=[
            pltpu.VMEM((2 * pps, MOBA_BLOCK, LANES), BF16),
            pltpu.VMEM((2 * pps, HEAD_DIM, MOBA_BLOCK), F32),
            pltpu.VMEM((2 * pps, MOBA_BLOCK, MOBA_BLOCK), F32),
            pltpu.VMEM((2, 2 * pps, MOBA_BLOCK, MOBA_BLOCK), BF16),
        ],
        compiler_params=pltpu.CompilerParams(
            dimension_semantics=("parallel", "parallel", "arbitrary"), vmem_limit_bytes=VMEM_LIMIT),
        name="moba_attn",
    )(bounded, qn, kn, vt5, sel)


def _ssd_kernel(xbc_ref, z_ref, dt_ref, cw_ref, cb_ref, dtb_ref, alog_ref, dsk_ref, ng_ref, ec_ref, ew_ref,
                o_ref, ext_ref, act_ref, state_ref, y_ref, acst_ref, ldtt_ref, rhs_ref, ly_ref, ls_ref):
    c = pl.program_id(1)
    L = SSM_CHUNK
    n_blk = act_ref.shape[0]
    b_blk = SSM_PAIRS
    c_blk = SSM_PAIRS + SSM_GROUPS

    @pl.when(c == 0)
    def _():
        ext_ref[:, 0:CONV_PAD, :] = jnp.zeros((n_blk, CONV_PAD, LANES), F32)
        state_ref[...] = jnp.zeros_like(state_ref)

    def conv_silu(blk):
        ext_ref[blk, CONV_PAD:CONV_PAD + L, :] = xbc_ref[:, blk * LANES:(blk + 1) * LANES].astype(F32)
        w = cw_ref[blk]
        conv = cb_ref[blk]
        for k in range(SSM_CONV):
            start = CONV_PAD - (SSM_CONV - 1) + k
            conv = conv + w[k:k + 1, :] * ext_ref[blk, start:start + L, :]
        ext_ref[blk, 0:CONV_PAD, :] = ext_ref[blk, L:L + CONV_PAD, :]
        act_ref[blk] = conv * _sigmoid(conv)

    dt_in = dt_ref[...] + dtb_ref[...]
    dtv = jnp.maximum(dt_in, 0.0) + jnp.log(1.0 + jnp.exp(-jnp.abs(dt_in)))
    a2 = dtv * (-LOG2E * jnp.exp(alog_ref[...]))
    row = lax.broadcasted_iota(jnp.int32, (L, L), 0)
    col = lax.broadcasted_iota(jnp.int32, (L, L), 1)
    tri = col <= row
    acs = jnp.dot(tri.astype(F32), a2, preferred_element_type=F32, precision=lax.Precision.HIGHEST)
    acst_ref[...] = acs.T
    ldtt_ref[...] = (jnp.log(dtv) * LOG2E).T
    lane = lax.broadcasted_iota(jnp.int32, (1, LANES), 1)
    is_a = lane < HEAD_DIM
    zero = jnp.zeros((), BF16)

    def split3(v):
        hi = v.astype(BF16)
        r1 = v - hi.astype(F32)
        mid = r1.astype(BF16)
        lo = (r1 - mid.astype(F32)).astype(BF16)
        return jnp.where(lane < SSM_HEADS, hi, jnp.where(lane < 2 * SSM_HEADS, mid, lo))

    acs_parts = split3(acs)
    wst_parts = split3(jnp.exp2(acs))
    ppg = SSM_PAIRS // SSM_GROUPS

    for g in range(SSM_GROUPS):
        for blk in list(range(g * ppg, (g + 1) * ppg)) + [b_blk + g, c_blk + g]:
            conv_silu(blk)
        bg = act_ref[b_blk + g]
        cbm = _nt_dot(act_ref[c_blk + g].astype(BF16), bg.astype(BF16))
        bt = bg.T
        cmat = jnp.dot(acs_parts, ec_ref[g], preferred_element_type=F32)
        wst = jnp.dot(wst_parts, ew_ref[g], preferred_element_type=F32)
        prev = jnp.concatenate([state_ref[g * ppg + pj] for pj in range(ppg)], axis=1).astype(BF16)
        y_off = jnp.dot(act_ref[c_blk + g].astype(BF16), prev, preferred_element_type=F32) * wst
        decs = []
        for pj in range(ppg):
            xp = act_ref[g * ppg + pj].astype(BF16)
            rhs_ref[g * ppg + pj] = jnp.concatenate([jnp.where(is_a, xp, zero), jnp.where(is_a, zero, xp)], axis=0)
            lhs_y, lhs_s, dec = [], [], []
            for hh in range(2):
                k = 2 * pj + hh
                h = g * 2 * ppg + k
                arow = acst_ref[h:h + 1, :]
                rrow = arow - ldtt_ref[h:h + 1, :]
                expo = jnp.where(tri, cmat[:, k * LANES:(k + 1) * LANES] - rrow, NEG)
                lhs_y.append((jnp.exp2(expo) * cbm).astype(BF16))
                tot = arow[:, L - 1:L]
                lhs_s.append((bt * jnp.exp2(tot - rrow)).astype(BF16))
                dec.append(jnp.exp2(tot))
            ly_ref[g * ppg + pj] = jnp.concatenate(lhs_y, axis=1)
            ls_ref[g * ppg + pj] = jnp.concatenate(lhs_s, axis=1)
            decs.append(jnp.where(is_a, dec[0], dec[1]))
        for pj in range(ppg):
            y_ref[g * ppg + pj] = (jnp.dot(ly_ref[g * ppg + pj], rhs_ref[g * ppg + pj], preferred_element_type=F32)
                                   + y_off[:, pj * LANES:(pj + 1) * LANES])
        for pj in range(ppg):
            j = g * ppg + pj
            state_ref[j] = state_ref[j] * decs[pj] + jnp.dot(ls_ref[g * ppg + pj], rhs_ref[g * ppg + pj],
                                                            preferred_element_type=F32)

        ys = []
        ssq = jnp.zeros((L, 1), F32)
        for j in range(g * ppg, (g + 1) * ppg):
            cols = slice(j * LANES, (j + 1) * LANES)
            zf = z_ref[:, cols].astype(F32)
            y = (y_ref[j] + dsk_ref[:, cols] * act_ref[j]) * (zf * _sigmoid(zf))
            ssq = ssq + jnp.sum(y * y, axis=-1, keepdims=True)
            ys.append(y)
        scale = lax.rsqrt(ssq * (1.0 / (ppg * LANES)) + EPS)
        for j, y in zip(range(g * ppg, (g + 1) * ppg), ys):
            cols = slice(j * LANES, (j + 1) * LANES)
            o_ref[:, cols] = (y * scale * ng_ref[:, cols]).astype(o_ref.dtype)


def _head_expansion(width):
    r = lax.broadcasted_iota(jnp.int32, (LANES, SSM_HEADS * width), 0)
    col = lax.broadcasted_iota(jnp.int32, (LANES, SSM_HEADS * width), 1)
    e = jnp.logical_and(r < 3 * SSM_HEADS, r % SSM_HEADS == col // width).astype(BF16)
    return e.reshape(LANES, SSM_GROUPS, -1).transpose(1, 0, 2)


def _ssd(proj3, dt3, conv_w, conv_b, dt_bias, a_log, d_skip_x, norm_g):
    b, s, _ = proj3.shape
    L = SSM_CHUNK
    inner = SSM_HEADS * HEAD_DIM
    conv_dim = inner + 2 * SSM_GROUPS * SSM_STATE
    n_blk = conv_dim // LANES
    hpg = SSM_HEADS // SSM_GROUPS
    ppg = SSM_PAIRS // SSM_GROUPS
    vec = lambda w: pl.BlockSpec((1, w), lambda bb, c: (0, 0))
    return pl.pallas_call(
        _ssd_kernel,
        grid=(b, s // L),
        in_specs=[
            pl.BlockSpec((None, L, conv_dim), lambda bb, c: (bb, c, 1)),
            pl.BlockSpec((None, L, inner), lambda bb, c: (bb, c, 3)),
            pl.BlockSpec((None, L, LANES), lambda bb, c: (bb, c, 0)),
            pl.BlockSpec((n_blk, SSM_CONV, LANES), lambda bb, c: (0, 0, 0)),
            pl.BlockSpec((n_blk, 1, LANES), lambda bb, c: (0, 0, 0)),
            vec(LANES), vec(LANES), vec(inner), vec(inner),
            pl.BlockSpec((SSM_GROUPS, LANES, hpg * L), lambda bb, c: (0, 0, 0)),
            pl.BlockSpec((SSM_GROUPS, LANES, hpg * HEAD_DIM), lambda bb, c: (0, 0, 0)),
        ],
        out_specs=pl.BlockSpec((None, L, inner), lambda bb, c: (bb, c, 0)),
        out_shape=jax.ShapeDtypeStruct((b, s, inner), BF16),
        scratch_shapes=[
            pltpu.VMEM((n_blk, L + CONV_PAD, LANES), F32),
            pltpu.VMEM((n_blk, L, LANES), F32),
            pltpu.VMEM((SSM_PAIRS, SSM_STATE, LANES), F32),
            pltpu.VMEM((SSM_PAIRS, L, LANES), F32),
            pltpu.VMEM((LANES, L), F32),
            pltpu.VMEM((LANES, L), F32),
            pltpu.VMEM((SSM_PAIRS, 2 * L, LANES), BF16),
            pltpu.VMEM((SSM_PAIRS, L, 2 * L), BF16),
            pltpu.VMEM((SSM_PAIRS, SSM_STATE, 2 * L), BF16),
        ],
        compiler_params=pltpu.CompilerParams(
            dimension_semantics=("parallel", "arbitrary"), vmem_limit_bytes=VMEM_LIMIT),
        name="ssd",
    )(proj3, proj3, dt3, conv_w, conv_b, dt_bias, a_log, d_skip_x, norm_g,
      _head_expansion(L), _head_expansion(HEAD_DIM))


def _mix_kernel(att_ref, yn_ref, ga_ref, gb_ref, x_ref, woa_ref, wos_ref, wout_ref, o_ref):
    ya = jnp.dot(att_ref[...], woa_ref[...], preferred_element_type=F32)
    yb = jnp.dot(yn_ref[...], wos_ref[...], preferred_element_type=F32)
    merged = _sigmoid(ga_ref[...].astype(F32)) * ya + _sigmoid(gb_ref[...].astype(F32)) * yb
    o_ref[...] = x_ref[...] + jnp.dot(merged.astype(BF16), wout_ref[...], preferred_element_type=F32)


def _mix_out(att2d, yn2d, proj2d, x2d, w_oa, w_os, w_out, tm=512):
    m, d = x2d.shape
    ga_col = (proj2d.shape[1] - 2 * d) // d
    row = lambda w, col=0: pl.BlockSpec((tm, w), lambda i, col=col: (i, col))
    full = lambda a: pl.BlockSpec(a.shape, lambda i: (0, 0))
    return pl.pallas_call(
        _mix_kernel,
        grid=(m // tm,),
        in_specs=[row(att2d.shape[1]), row(yn2d.shape[1]), row(d, ga_col), row(d, ga_col + 1), row(d),
                  full(w_oa), full(w_os), full(w_out)],
        out_specs=row(d),
        out_shape=jax.ShapeDtypeStruct((m, d), F32),
        compiler_params=pltpu.CompilerParams(
            dimension_semantics=("parallel",), vmem_limit_bytes=VMEM_LIMIT),
        name="mix_out",
    )(att2d, yn2d, proj2d, proj2d, x2d, w_oa, w_os, w_out)


def _rms(x, g):
    return x * lax.rsqrt(jnp.mean(x * x, axis=-1, keepdims=True) + EPS) * g


def _ffn_kernel(x_ref, g2_ref, wg_ref, wu_ref, wd_ref, p_ref, g3_ref, wpg_ref, wpp_ref, o_ref,
                h_ref, acc_ref):
    j = pl.program_id(1)

    @pl.when(j == 0)
    def _():
        h_ref[...] = _rms(x_ref[...], g2_ref[...]).astype(BF16)
        acc_ref[...] = jnp.zeros_like(acc_ref)

    h = h_ref[...]
    gate = jnp.dot(h, wg_ref[...], preferred_element_type=F32)
    up = jnp.dot(h, wu_ref[...], preferred_element_type=F32)
    act = (gate * _sigmoid(gate) * up).astype(BF16)
    acc_ref[...] += jnp.dot(act, wd_ref[...], preferred_element_type=F32)

    @pl.when(j == pl.num_programs(1) - 1)
    def _():
        x2 = x_ref[...] + acc_ref[...]
        h3 = _rms(x2, g3_ref[...]).astype(BF16)
        pgate = _sigmoid(jnp.dot(h3, wpg_ref[...], preferred_element_type=F32))
        proj = jnp.dot(p_ref[...].astype(BF16), wpp_ref[...], preferred_element_type=F32)
        o_ref[...] = x2 + proj * pgate


def _ffn_ple(x2d, g2, w_gu, w_down, p2d, g3, w_pg, w_pp, tm=512, n_ff=2):
    m, d = x2d.shape
    d_ff = w_down.shape[0]
    tf = d_ff // n_ff
    const = lambda a: pl.BlockSpec(a.shape, lambda i, j: (0, 0))
    return pl.pallas_call(
        _ffn_kernel,
        grid=(m // tm, n_ff),
        in_specs=[
            pl.BlockSpec((tm, d), lambda i, j: (i, 0)),
            const(g2),
            pl.BlockSpec((d, tf), lambda i, j: (0, j)),
            pl.BlockSpec((d, tf), lambda i, j: (0, n_ff + j)),
            pl.BlockSpec((tf, d), lambda i, j: (j, 0)),
            pl.BlockSpec((tm, p2d.shape[1]), lambda i, j: (i, 0)),
            const(g3), const(w_pg), const(w_pp),
        ],
        out_specs=pl.BlockSpec((tm, d), lambda i, j: (i, 0)),
        out_shape=jax.ShapeDtypeStruct((m, d), F32),
        scratch_shapes=[pltpu.VMEM((tm, d), BF16), pltpu.VMEM((tm, d), F32)],
        compiler_params=pltpu.CompilerParams(
            dimension_semantics=("parallel", "arbitrary"), vmem_limit_bytes=VMEM_LIMIT),
        name="ffn_ple",
    )(x2d, g2, w_gu, w_gu, w_down, p2d, g3, w_pg, w_pp)


def _head_lanes(v):
    return jnp.pad(jnp.tile(v.astype(F32), 3), (0, LANES - 3 * SSM_HEADS)).reshape(1, LANES)


def _lane_blocks(a):
    r, w = a.shape
    return a.astype(F32).reshape(r, w // LANES, LANES).transpose(1, 0, 2)


def kernel(x, p, ln1_g, w_in, q_norm_g, k_norm_g, w_o_attn, conv_w, conv_b, dt_bias, a_log, d_skip,
           ssm_norm_g, w_o_ssm, w_out, ln2_g, w_gate_up, w_down, ln3_g, w_ple_gate, w_ple_proj):
    b, s, d = x.shape
    assert s % MOBA_BLOCK == 0 and s // MOBA_BLOCK <= MAX_BLOCKS
    attn_w = ATTN_HEADS * HEAD_DIM
    inner = SSM_HEADS * HEAD_DIM
    conv_dim = inner + 2 * SSM_GROUPS * SSM_STATE
    off_z = 3 * attn_w
    off_xbc = off_z + inner
    off_dt = off_xbc + conv_dim
    off_g = off_dt + SSM_HEADS
    row = lambda v: v.astype(F32).reshape(1, -1)

    for i in range(w_in.shape[0]):
        wi = w_in[i]
        w_segs = [wi[:, :off_z].astype(BF16), wi[:, off_xbc:off_dt].astype(BF16),
                  wi[:, off_z:off_xbc].astype(BF16), wi[:, off_g:].astype(BF16)]
        w_dt = jnp.pad(jnp.tile(wi[:, off_dt:off_g], (1, 3)), ((0, 0), (0, LANES - 3 * SSM_HEADS))).astype(BF16)

        x2d = x.reshape(b * s, d)
        proj, dt_raw = _in_proj(x2d, row(ln1_g[i]), w_segs, w_dt)
        proj3 = proj.reshape(b, s, -1)

        qn, kn, sel, vt5 = _attn_prep(
            proj3, row(jnp.tile(q_norm_g[i], ATTN_HEADS)), row(jnp.tile(k_norm_g[i], ATTN_HEADS)))
        score_bound = 1.02 * LOG2E * HEAD_DIM ** 0.5 * jnp.max(jnp.abs(q_norm_g[i])) * jnp.max(jnp.abs(k_norm_g[i]))
        bounded = (2.0 * score_bound <= SAFE_EXP2_RANGE).astype(jnp.int32).reshape(1)
        att = _moba_attention(bounded, qn, kn, sel, vt5)

        yn = _ssd(proj3, dt_raw.reshape(b, s, LANES), _lane_blocks(conv_w[i]), _lane_blocks(row(conv_b[i])),
                  _head_lanes(dt_bias[i]), _head_lanes(a_log[i]), row(jnp.repeat(d_skip[i], HEAD_DIM)),
                  row(ssm_norm_g[i]))

        x1 = _mix_out(att.reshape(b * s, attn_w), yn.reshape(b * s, inner), proj, x2d,
                      w_o_attn[i].astype(BF16), w_o_ssm[i].astype(BF16), w_out[i].astype(BF16))
        x2 = _ffn_ple(x1, row(ln2_g[i]), w_gate_up[i].astype(BF16), w_down[i].astype(BF16),
                      p[i].reshape(b * s, -1), row(ln3_g[i]), w_ple_gate[i].astype(BF16),
                      w_ple_proj[i].astype(BF16))
        x = x2.reshape(b, s, d)
    return x
```

```python
import math

import jax
import jax.numpy as jnp
from jax import lax
from jax.experimental import pallas as pl
from jax.experimental.pallas import tpu as pltpu

F32 = jnp.float32
BF16 = jnp.bfloat16

EPS = 1e-6
LOG2E = math.log2(math.e)
LANES = 128
HEAD_DIM = 64
ATTN_HEADS = 16
HEAD_PAIRS = ATTN_HEADS // 2
ATTN_PAIRS_PER_STEP = 8
MOBA_BLOCK = 256
MOBA_TOPK = 3
MAX_BLOCKS = 32
SSM_HEADS = 32
SSM_PAIRS = SSM_HEADS // 2
SSM_GROUPS = 4
SSM_STATE = 128
SSM_CONV = 4
SSM_CHUNK = 128
CONV_PAD = 8
NEG = -1e30
SAFE_EXP2_RANGE = 60.0
VMEM_LIMIT = 56 * 1024 * 1024


def _sigmoid(v):
    return 1.0 / (1.0 + jnp.exp(-v))


def _nt_dot(a, b, **kw):
    return lax.dot_general(a, b, (((1,), (1,)), ((), ())), preferred_element_type=F32, **kw)


def _inproj_kernel(x_ref, g_ref, w_ref, wdt_ref, o_ref, dt_ref, h_ref):
    @pl.when(pl.program_id(1) == 0)
    def _():
        x = x_ref[...]
        ms = jnp.mean(x * x, axis=-1, keepdims=True)
        h = (x * lax.rsqrt(ms + EPS) * g_ref[...]).astype(BF16)
        h_ref[...] = h
        dt_ref[...] = jnp.dot(h, wdt_ref[...], preferred_element_type=F32)

    o_ref[...] = jnp.dot(h_ref[...], w_ref[...], preferred_element_type=F32).astype(o_ref.dtype)


def _in_proj(x2d, g, w_main, w_dt, tm=1024, tn=1024):
    m, d = x2d.shape
    n = w_main.shape[1]
    return pl.pallas_call(
        _inproj_kernel,
        grid=(m // tm, n // tn),
        in_specs=[
            pl.BlockSpec((tm, d), lambda i, j: (i, 0)),
            pl.BlockSpec((1, d), lambda i, j: (0, 0)),
            pl.BlockSpec((d, tn), lambda i, j: (0, j)),
            pl.BlockSpec((d, LANES), lambda i, j: (0, 0)),
        ],
        out_specs=[
            pl.BlockSpec((tm, tn), lambda i, j: (i, j)),
            pl.BlockSpec((tm, LANES), lambda i, j: (i, 0)),
        ],
        out_shape=[jax.ShapeDtypeStruct((m, n), BF16), jax.ShapeDtypeStruct((m, LANES), F32)],
        scratch_shapes=[pltpu.VMEM((tm, d), BF16)],
        compiler_params=pltpu.CompilerParams(
            dimension_semantics=("parallel", "arbitrary"), vmem_limit_bytes=VMEM_LIMIT),
        name="in_proj",
    )(x2d, g, w_main, w_dt)


def _head_rms_norm(x, gain, is_a):
    outs = []
    for c in range(x.shape[1] // LANES):
        blk = x[:, c * LANES:(c + 1) * LANES]
        sq = blk * blk
        sa = jnp.sum(jnp.where(is_a, sq, 0.0), axis=-1, keepdims=True)
        sb = jnp.sum(jnp.where(is_a, 0.0, sq), axis=-1, keepdims=True)
        ms = jnp.where(is_a, sa, sb) * (1.0 / HEAD_DIM)
        outs.append(blk * lax.rsqrt(ms + EPS))
    return jnp.concatenate(outs, axis=1) * gain


def _prep_kernel(q_ref, k_ref, v_ref, qg_ref, kg_ref, qn_ref, kn_ref, sel_ref, vt_ref, kmean_ref):
    i = pl.program_id(1)

    @pl.when(i == 0)
    def _():
        kmean_ref[...] = jnp.zeros_like(kmean_ref)

    lane = lax.broadcasted_iota(jnp.int32, (1, LANES), 1)
    is_a = lane < HEAD_DIM
    qn = _head_rms_norm(q_ref[...].astype(F32), qg_ref[...], is_a) * (HEAD_DIM ** -0.5 * LOG2E)
    kn = _head_rms_norm(k_ref[...].astype(F32), kg_ref[...], is_a)
    kn_ref[...] = kn.astype(BF16)

    vt = v_ref[...].astype(F32).T
    vt_ref[...] = vt.reshape(vt_ref.shape).astype(BF16)

    kmean = kmean_ref[...]
    q_hi = qn.astype(BF16)
    qn_ref[...] = q_hi
    q_lo = (qn - q_hi.astype(F32)).astype(BF16)
    km_hi = kmean.astype(BF16)
    km_lo = (kmean - km_hi.astype(F32)).astype(BF16)
    zero = jnp.zeros((), BF16)
    jidx = lax.broadcasted_iota(jnp.int32, (MAX_BLOCKS, MOBA_BLOCK), 0).astype(F32)
    valid = jidx < i.astype(F32)
    for p in range(HEAD_PAIRS):
        cols = slice(p * LANES, (p + 1) * LANES)
        km4 = jnp.concatenate([jnp.where(is_a, km_hi[:, cols], zero), jnp.where(is_a, zero, km_hi[:, cols]),
                               jnp.where(is_a, km_lo[:, cols], zero), jnp.where(is_a, zero, km_lo[:, cols])],
                              axis=0)
        g_hi = _nt_dot(km4, q_hi[:, cols])
        g_lo = _nt_dot(km4[:2 * MAX_BLOCKS], q_lo[:, cols])
        for hh in range(2):
            r0 = hh * MAX_BLOCKS
            gate = (g_hi[r0:r0 + MAX_BLOCKS] + g_hi[r0 + 2 * MAX_BLOCKS:r0 + 3 * MAX_BLOCKS]
                    + g_lo[r0:r0 + MAX_BLOCKS])
            gate = jnp.where(valid, gate, -jnp.inf)
            sel = jnp.zeros_like(gate)
            for _ in range(MOBA_TOPK):
                top = jnp.max(gate, axis=0, keepdims=True)
                first = jnp.min(jnp.where(gate == top, jidx, float(MAX_BLOCKS)), axis=0, keepdims=True)
                pick = jidx == first
                sel = jnp.where(pick, 1.0, sel)
                gate = jnp.where(pick, -jnp.inf, gate)
            sel_ref[2 * p + hh] = jnp.where(valid, sel, 0.0)

    kmean_ref[pl.ds(i, 1), :] = jnp.mean(kn, axis=0, keepdims=True)


def _attn_prep(proj3, qg, kg):
    b, s, _ = proj3.shape
    nb = s // MOBA_BLOCK
    width = ATTN_HEADS * HEAD_DIM
    tile = lambda col: pl.BlockSpec((None, MOBA_BLOCK, width), lambda bb, i, col=col: (bb, i, col))
    gain = pl.BlockSpec((1, width), lambda bb, i: (0, 0))
    return pl.pallas_call(
        _prep_kernel,
        grid=(b, nb),
        in_specs=[tile(0), tile(1), tile(2), gain, gain],
        out_specs=[
            tile(0), tile(0),
            pl.BlockSpec((None, ATTN_HEADS, None, MAX_BLOCKS, MOBA_BLOCK), lambda bb, i: (bb, 0, i, 0, 0)),
            pl.BlockSpec((None, HEAD_PAIRS, None, LANES, MOBA_BLOCK), lambda bb, i: (bb, 0, i, 0, 0)),
        ],
        out_shape=[
            jax.ShapeDtypeStruct((b, s, width), BF16),
            jax.ShapeDtypeStruct((b, s, width), BF16),
            jax.ShapeDtypeStruct((b, ATTN_HEADS, nb, MAX_BLOCKS, MOBA_BLOCK), F32),
            jax.ShapeDtypeStruct((b, HEAD_PAIRS, nb, LANES, MOBA_BLOCK), BF16),
        ],
        scratch_shapes=[pltpu.VMEM((MAX_BLOCKS, width), F32)],
        compiler_params=pltpu.CompilerParams(
            dimension_semantics=("parallel", "arbitrary"), vmem_limit_bytes=VMEM_LIMIT),
        name="attn_prep",
    )(proj3, proj3, proj3, qg, kg)


def _attn_kernel(bounded_ref, q_ref, k_ref, vt_ref, sel_ref, o_ref, qm_ref, acc_ref, s_ref, p_ref):
    i = pl.program_id(2)
    heads = 2 * ATTN_PAIRS_PER_STEP
    lane = lax.broadcasted_iota(jnp.int32, (1, LANES), 1)
    is_a = lane < HEAD_DIM
    zero = jnp.zeros((), BF16)
    for pp in range(ATTN_PAIRS_PER_STEP):
        qp = q_ref[:, pp * LANES:(pp + 1) * LANES]
        qm_ref[2 * pp] = jnp.where(is_a, qp, zero)
        qm_ref[2 * pp + 1] = jnp.where(is_a, zero, qp)

    def scores(j, h):
        pp = h // 2
        kt = k_ref[pl.ds(pl.multiple_of(j * MOBA_BLOCK, MOBA_BLOCK), MOBA_BLOCK), pp * LANES:(pp + 1) * LANES]
        return _nt_dot(kt, qm_ref[h])

    def values(j, h):
        return vt_ref[h // 2, j, (h % 2) * HEAD_DIM:(h % 2 + 1) * HEAD_DIM, :]

    krow = lax.broadcasted_iota(jnp.int32, (MOBA_BLOCK, MOBA_BLOCK), 0)
    qcol = lax.broadcasted_iota(jnp.int32, (MOBA_BLOCK, MOBA_BLOCK), 1)
    causal = krow <= qcol
    zero_row = jnp.zeros((1, MOBA_BLOCK), F32)

    def bounded_scores():
        ls = [zero_row] * heads
        for h in range(heads):
            p = jnp.exp2(jnp.where(causal, scores(i, h), NEG))
            ls[h] = jnp.sum(p, axis=0, keepdims=True)
            p_ref[0, h] = p.astype(BF16)
        for h in range(heads):
            acc_ref[h] = jnp.dot(values(i, h), p_ref[0, h], preferred_element_type=F32)

        def body(jj, ls):
            ls = list(ls)
            for u in range(2):
                j = 2 * jj + u
                for h in range(heads):
                    picked = sel_ref[h, pl.ds(j, 1), :] > 0.0
                    if u == 1:
                        picked = jnp.logical_and(picked, j < i)
                    p = jnp.exp2(scores(j, h) + jnp.where(picked, 0.0, NEG))
                    ls[h] = ls[h] + jnp.sum(p, axis=0, keepdims=True)
                    p_ref[u, h] = p.astype(BF16)
            for u in range(2):
                for h in range(heads):
                    acc_ref[h] += jnp.dot(values(2 * jj + u, h), p_ref[u, h], preferred_element_type=F32)
            return tuple(ls)

        return lax.fori_loop(0, (i + 1) // 2, body, tuple(ls))

    def running_max():
        m0, l0 = [], []
        for h in range(heads):
            s = jnp.where(causal, scores(i, h), NEG)
            m0.append(jnp.max(s, axis=0, keepdims=True))
            s_ref[h] = s
        for h in range(heads):
            p = jnp.exp2(s_ref[h] - m0[h])
            l0.append(jnp.sum(p, axis=0, keepdims=True))
            p_ref[0, h] = p.astype(BF16)
        for h in range(heads):
            acc_ref[h] = jnp.dot(values(i, h), p_ref[0, h], preferred_element_type=F32)

        def body(j, carry):
            ms, ls = carry
            tile_max, new_m, new_l, alphas = [], [], [], []
            for h in range(heads):
                s = scores(j, h)
                tile_max.append(jnp.max(s, axis=0, keepdims=True))
                s_ref[h] = s
            for h in range(heads):
                picked = sel_ref[h, pl.ds(j, 1), :] > 0.0
                m = jnp.maximum(ms[h], jnp.where(picked, tile_max[h], NEG))
                p = jnp.exp2(s_ref[h] - jnp.where(picked, m, -NEG))
                alpha = jnp.exp2(ms[h] - m)
                new_m.append(m)
                new_l.append(alpha * ls[h] + jnp.sum(p, axis=0, keepdims=True))
                alphas.append(alpha)
                p_ref[0, h] = p.astype(BF16)
            for h in range(heads):
                acc_ref[h] = alphas[h] * acc_ref[h] + jnp.dot(values(j, h), p_ref[0, h],
                                                              preferred_element_type=F32)
            return tuple(new_m), tuple(new_l)

        return lax.fori_loop(0, i, body, (tuple(m0), tuple(l0)))[1]

    ls = lax.cond(bounded_ref[0] != 0, bounded_scores, running_max)
    for pp in range(ATTN_PAIRS_PER_STEP):
        out_t = jnp.concatenate([acc_ref[2 * pp] * (1.0 / ls[2 * pp]),
                                 acc_ref[2 * pp + 1] * (1.0 / ls[2 * pp + 1])], axis=0)
        o_ref[:, pp * LANES:(pp + 1) * LANES] = out_t.T.astype(o_ref.dtype)


def _moba_attention(bounded, qn, kn, sel, vt5):
    b, s, width = qn.shape
    nb = s // MOBA_BLOCK
    pps = ATTN_PAIRS_PER_STEP
    gw = pps * LANES
    qtile = pl.BlockSpec((None, MOBA_BLOCK, gw), lambda bb, g, i: (bb, i, g))
    return pl.pallas_call(
        _attn_kernel,
        grid=(b, HEAD_PAIRS // pps, nb),
        in_specs=[
            pl.BlockSpec(memory_space=pltpu.SMEM),
            qtile,
            pl.BlockSpec((None, s, gw), lambda bb, g, i: (bb, 0, g), pipeline_mode=pl.Buffered(1)),
            pl.BlockSpec((None, pps, nb, LANES, MOBA_BLOCK), lambda bb, g, i: (bb, g, 0, 0, 0),
                         pipeline_mode=pl.Buffered(1)),
            pl.BlockSpec((None, 2 * pps, None, MAX_BLOCKS, MOBA_BLOCK), lambda bb, g, i: (bb, g, i, 0, 0)),
        ],
        out_specs=qtile,
        out_shape=jax.ShapeDtypeStruct((b, s, width), BF16),
        scratch_shapes=[
            pltpu.VMEM((2 * pps, MOBA_BLOCK, LANES), BF16),
            pltpu.VMEM((2 * pps, HEAD_DIM, MOBA_BLOCK), F32),
            pltpu.VMEM((2 * pps, MOBA_BLOCK, MOBA_BLOCK), F32),
            pltpu.VMEM((2, 2 * pps, MOBA_BLOCK, MOBA_BLOCK), BF16),
        ],
        compiler_params=pltpu.CompilerParams(
            dimension_semantics=("parallel", "parallel", "arbitrary"), vmem_limit_bytes=VMEM_LIMIT),
        name="moba_attn",
    )(bounded, qn, kn, vt5, sel)


def _ssd_kernel(xbc_ref, z_ref, dt_ref, cw_ref, cb_ref, dtb_ref, alog_ref, dsk_ref, ng_ref, ec_ref, ew_ref,
                o_ref, ext_ref, act_ref, state_ref, y_ref, acst_ref, ldtt_ref, rhs_ref, ly_ref, ls_ref):
    c = pl.program_id(1)
    L = SSM_CHUNK
    n_blk = act_ref.shape[0]
    b_blk = SSM_PAIRS
    c_blk = SSM_PAIRS + SSM_GROUPS

    @pl.when(c == 0)
    def _():
        ext_ref[:, 0:CONV_PAD, :] = jnp.zeros((n_blk, CONV_PAD, LANES), F32)
        state_ref[...] = jnp.zeros_like(state_ref)

    def conv_silu(blk):
        ext_ref[blk, CONV_PAD:CONV_PAD + L, :] = xbc_ref[:, blk * LANES:(blk + 1) * LANES].astype(F32)
        w = cw_ref[blk]
        conv = cb_ref[blk]
        for k in range(SSM_CONV):
            start = CONV_PAD - (SSM_CONV - 1) + k
            conv = conv + w[k:k + 1, :] * ext_ref[blk, start:start + L, :]
        ext_ref[blk, 0:CONV_PAD, :] = ext_ref[blk, L:L + CONV_PAD, :]
        act_ref[blk] = conv * _sigmoid(conv)

    dt_in = dt_ref[...] + dtb_ref[...]
    dtv = jnp.maximum(dt_in, 0.0) + jnp.log(1.0 + jnp.exp(-jnp.abs(dt_in)))
    a2 = dtv * (-LOG2E * jnp.exp(alog_ref[...]))
    row = lax.broadcasted_iota(jnp.int32, (L, L), 0)
    col = lax.broadcasted_iota(jnp.int32, (L, L), 1)
    tri = col <= row
    acs = jnp.dot(tri.astype(F32), a2, preferred_element_type=F32, precision=lax.Precision.HIGHEST)
    acst_ref[...] = acs.T
    ldtt_ref[...] = (jnp.log(dtv) * LOG2E).T
    lane = lax.broadcasted_iota(jnp.int32, (1, LANES), 1)
    is_a = lane < HEAD_DIM
    zero = jnp.zeros((), BF16)

    def split3(v):
        hi = v.astype(BF16)
        r1 = v - hi.astype(F32)
        mid = r1.astype(BF16)
        lo = (r1 - mid.astype(F32)).astype(BF16)
        return jnp.where(lane < SSM_HEADS, hi, jnp.where(lane < 2 * SSM_HEADS, mid, lo))

    acs_parts = split3(acs)
    wst_parts = split3(jnp.exp2(acs))
    ppg = SSM_PAIRS // SSM_GROUPS

    for g in range(SSM_GROUPS):
        for blk in list(range(g * ppg, (g + 1) * ppg)) + [b_blk + g, c_blk + g]:
            conv_silu(blk)
        bg = act_ref[b_blk + g]
        cbm = _nt_dot(act_ref[c_blk + g].astype(BF16), bg.astype(BF16))
        bt = bg.T
        cmat = jnp.dot(acs_parts, ec_ref[g], preferred_element_type=F32)
        wst = jnp.dot(wst_parts, ew_ref[g], preferred_element_type=F32)
        prev = jnp.concatenate([state_ref[g * ppg + pj] for pj in range(ppg)], axis=1).astype(BF16)
        y_off = jnp.dot(act_ref[c_blk + g].astype(BF16), prev, preferred_element_type=F32) * wst
        decs = []
        for pj in range(ppg):
            xp = act_ref[g * ppg + pj].astype(BF16)
            rhs_ref[g * ppg + pj] = jnp.concatenate([jnp.where(is_a, xp, zero), jnp.where(is_a, zero, xp)], axis=0)
            lhs_y, lhs_s, dec = [], [], []
            for hh in range(2):
                k = 2 * pj + hh
                h = g * 2 * ppg + k
                arow = acst_ref[h:h + 1, :]
                rrow = arow - ldtt_ref[h:h + 1, :]
                expo = jnp.where(tri, cmat[:, k * LANES:(k + 1) * LANES] - rrow, NEG)
                lhs_y.append((jnp.exp2(expo) * cbm).astype(BF16))
                tot = arow[:, L - 1:L]
                lhs_s.append((bt * jnp.exp2(tot - rrow)).astype(BF16))
                dec.append(jnp.exp2(tot))
            ly_ref[g * ppg + pj] = jnp.concatenate(lhs_y, axis=1)
            ls_ref[g * ppg + pj] = jnp.concatenate(lhs_s, axis=1)
            decs.append(jnp.where(is_a, dec[0], dec[1]))
        for pj in range(ppg):
            y_ref[g * ppg + pj] = (jnp.dot(ly_ref[g * ppg + pj], rhs_ref[g * ppg + pj], preferred_element_type=F32)
                                   + y_off[:, pj * LANES:(pj + 1) * LANES])
        for pj in range(ppg):
            j = g * ppg + pj
            state_ref[j] = state_ref[j] * decs[pj] + jnp.dot(ls_ref[g * ppg + pj], rhs_ref[g * ppg + pj],
                                                            preferred_element_type=F32)

        ys = []
        ssq = jnp.zeros((L, 1), F32)
        for j in range(g * ppg, (g + 1) * ppg):
            cols = slice(j * LANES, (j + 1) * LANES)
            zf = z_ref[:, cols].astype(F32)
            y = (y_ref[j] + dsk_ref[:, cols] * act_ref[j]) * (zf * _sigmoid(zf))
            ssq = ssq + jnp.sum(y * y, axis=-1, keepdims=True)
            ys.append(y)
        scale = lax.rsqrt(ssq * (1.0 / (ppg * LANES)) + EPS)
        for j, y in zip(range(g * ppg, (g + 1) * ppg), ys):
            cols = slice(j * LANES, (j + 1) * LANES)
            o_ref[:, cols] = (y * scale * ng_ref[:, cols]).astype(o_ref.dtype)


def _head_expansion(width):
    r = lax.broadcasted_iota(jnp.int32, (LANES, SSM_HEADS * width), 0)
    col = lax.broadcasted_iota(jnp.int32, (LANES, SSM_HEADS * width), 1)
    e = jnp.logical_and(r < 3 * SSM_HEADS, r % SSM_HEADS == col // width).astype(BF16)
    return e.reshape(LANES, SSM_GROUPS, -1).transpose(1, 0, 2)


def _ssd(proj3, dt3, conv_w, conv_b, dt_bias, a_log, d_skip_x, norm_g):
    b, s, _ = proj3.shape
    L = SSM_CHUNK
    inner = SSM_HEADS * HEAD_DIM
    conv_dim = inner + 2 * SSM_GROUPS * SSM_STATE
    n_blk = conv_dim // LANES
    hpg = SSM_HEADS // SSM_GROUPS
    ppg = SSM_PAIRS // SSM_GROUPS
    vec = lambda w: pl.BlockSpec((1, w), lambda bb, c: (0, 0))
    return pl.pallas_call(
        _ssd_kernel,
        grid=(b, s // L),
        in_specs=[
            pl.BlockSpec((None, L, conv_dim), lambda bb, c: (bb, c, 1)),
            pl.BlockSpec((None, L, inner), lambda bb, c: (bb, c, 3)),
            pl.BlockSpec((None, L, LANES), lambda bb, c: (bb, c, 0)),
            pl.BlockSpec((n_blk, SSM_CONV, LANES), lambda bb, c: (0, 0, 0)),
            pl.BlockSpec((n_blk, 1, LANES), lambda bb, c: (0, 0, 0)),
            vec(LANES), vec(LANES), vec(inner), vec(inner),
            pl.BlockSpec((SSM_GROUPS, LANES, hpg * L), lambda bb, c: (0, 0, 0)),
            pl.BlockSpec((SSM_GROUPS, LANES, hpg * HEAD_DIM), lambda bb, c: (0, 0, 0)),
        ],
        out_specs=pl.BlockSpec((None, L, inner), lambda bb, c: (bb, c, 0)),
        out_shape=jax.ShapeDtypeStruct((b, s, inner), BF16),
        scratch_shapes=[
            pltpu.VMEM((n_blk, L + CONV_PAD, LANES), F32),
            pltpu.VMEM((n_blk, L, LANES), F32),
            pltpu.VMEM((SSM_PAIRS, SSM_STATE, LANES), F32),
            pltpu.VMEM((SSM_PAIRS, L, LANES), F32),
            pltpu.VMEM((LANES, L), F32),
            pltpu.VMEM((LANES, L), F32),
            pltpu.VMEM((SSM_PAIRS, 2 * L, LANES), BF16),
            pltpu.VMEM((SSM_PAIRS, L, 2 * L), BF16),
            pltpu.VMEM((SSM_PAIRS, SSM_STATE, 2 * L), BF16),
        ],
        compiler_params=pltpu.CompilerParams(
            dimension_semantics=("parallel", "arbitrary"), vmem_limit_bytes=VMEM_LIMIT),
        name="ssd",
    )(proj3, proj3, dt3, conv_w, conv_b, dt_bias, a_log, d_skip_x, norm_g,
      _head_expansion(L), _head_expansion(HEAD_DIM))


def _mix_kernel(att_ref, yn_ref, ga_ref, gb_ref, x_ref, woa_ref, wos_ref, wout_ref, o_ref):
    ya = jnp.dot(att_ref[...], woa_ref[...], preferred_element_type=F32)
    yb = jnp.dot(yn_ref[...], wos_ref[...], preferred_element_type=F32)
    merged = _sigmoid(ga_ref[...].astype(F32)) * ya + _sigmoid(gb_ref[...].astype(F32)) * yb
    o_ref[...] = x_ref[...] + jnp.dot(merged.astype(BF16), wout_ref[...], preferred_element_type=F32)


def _mix_out(att2d, yn2d, proj2d, x2d, w_oa, w_os, w_out, tm=512):
    m, d = x2d.shape
    ga_col = (proj2d.shape[1] - 2 * d) // d
    row = lambda w, col=0: pl.BlockSpec((tm, w), lambda i, col=col: (i, col))
    full = lambda a: pl.BlockSpec(a.shape, lambda i: (0, 0))
    return pl.pallas_call(
        _mix_kernel,
        grid=(m // tm,),
        in_specs=[row(att2d.shape[1]), row(yn2d.shape[1]), row(d, ga_col), row(d, ga_col + 1), row(d),
                  full(w_oa), full(w_os), full(w_out)],
        out_specs=row(d),
        out_shape=jax.ShapeDtypeStruct((m, d), F32),
        compiler_params=pltpu.CompilerParams(
            dimension_semantics=("parallel",), vmem_limit_bytes=VMEM_LIMIT),
        name="mix_out",
    )(att2d, yn2d, proj2d, proj2d, x2d, w_oa, w_os, w_out)


def _rms(x, g):
    return x * lax.rsqrt(jnp.mean(x * x, axis=-1, keepdims=True) + EPS) * g


def _ffn_kernel(x_ref, g2_ref, wg_ref, wu_ref, wd_ref, p_ref, g3_ref, wpg_ref, wpp_ref, o_ref,
                h_ref, acc_ref):
    j = pl.program_id(1)

    @pl.when(j == 0)
    def _():
        h_ref[...] = _rms(x_ref[...], g2_ref[...]).astype(BF16)
        acc_ref[...] = jnp.zeros_like(acc_ref)

    h = h_ref[...]
    gate = jnp.dot(h, wg_ref[...], preferred_element_type=F32)
    up = jnp.dot(h, wu_ref[...], preferred_element_type=F32)
    act = (gate * _sigmoid(gate) * up).astype(BF16)
    acc_ref[...] += jnp.dot(act, wd_ref[...], preferred_element_type=F32)

    @pl.when(j == pl.num_programs(1) - 1)
    def _():
        x2 = x_ref[...] + acc_ref[...]
        h3 = _rms(x2, g3_ref[...]).astype(BF16)
        pgate = _sigmoid(jnp.dot(h3, wpg_ref[...], preferred_element_type=F32))
        proj = jnp.dot(p_ref[...].astype(BF16), wpp_ref[...], preferred_element_type=F32)
        o_ref[...] = x2 + proj * pgate


def _ffn_ple(x2d, g2, w_gu, w_down, p2d, g3, w_pg, w_pp, tm=512, n_ff=2):
    m, d = x2d.shape
    d_ff = w_down.shape[0]
    tf = d_ff // n_ff
    const = lambda a: pl.BlockSpec(a.shape, lambda i, j: (0, 0))
    return pl.pallas_call(
        _ffn_kernel,
        grid=(m // tm, n_ff),
        in_specs=[
            pl.BlockSpec((tm, d), lambda i, j: (i, 0)),
            const(g2),
            pl.BlockSpec((d, tf), lambda i, j: (0, j)),
            pl.BlockSpec((d, tf), lambda i, j: (0, n_ff + j)),
            pl.BlockSpec((tf, d), lambda i, j: (j, 0)),
            pl.BlockSpec((tm, p2d.shape[1]), lambda i, j: (i, 0)),
            const(g3), const(w_pg), const(w_pp),
        ],
        out_specs=pl.BlockSpec((tm, d), lambda i, j: (i, 0)),
        out_shape=jax.ShapeDtypeStruct((m, d), F32),
        scratch_shapes=[pltpu.VMEM((tm, d), BF16), pltpu.VMEM((tm, d), F32)],
        compiler_params=pltpu.CompilerParams(
            dimension_semantics=("parallel", "arbitrary"), vmem_limit_bytes=VMEM_LIMIT),
        name="ffn_ple",
    )(x2d, g2, w_gu, w_gu, w_down, p2d, g3, w_pg, w_pp)


def _head_lanes(v):
    return jnp.pad(jnp.tile(v.astype(F32), 3), (0, LANES - 3 * SSM_HEADS)).reshape(1, LANES)


def _lane_blocks(a):
    r, w = a.shape
    return a.astype(F32).reshape(r, w // LANES, LANES).transpose(1, 0, 2)


def kernel(x, p, ln1_g, w_in, q_norm_g, k_norm_g, w_o_attn, conv_w, conv_b, dt_bias, a_log, d_skip,
           ssm_norm_g, w_o_ssm, w_out, ln2_g, w_gate_up, w_down, ln3_g, w_ple_gate, w_ple_proj):
    b, s, d = x.shape
    assert s % MOBA_BLOCK == 0 and s // MOBA_BLOCK <= MAX_BLOCKS
    attn_w = ATTN_HEADS * HEAD_DIM
    inner = SSM_HEADS * HEAD_DIM
    conv_dim = inner + 2 * SSM_GROUPS * SSM_STATE
    off_z = 3 * attn_w
    off_xbc = off_z + inner
    off_dt = off_xbc + conv_dim
    off_g = off_dt + SSM_HEADS
    row = lambda v: v.astype(F32).reshape(1, -1)

    for i in range(w_in.shape[0]):
        wi = w_in[i]
        w_main = jnp.concatenate(
            [wi[:, :off_z], wi[:, off_xbc:off_dt], wi[:, off_z:off_xbc], wi[:, off_g:]], axis=1).astype(BF16)
        w_dt = jnp.pad(jnp.tile(wi[:, off_dt:off_g], (1, 3)), ((0, 0), (0, LANES - 3 * SSM_HEADS))).astype(BF16)

        x2d = x.reshape(b * s, d)
        proj, dt_raw = _in_proj(x2d, row(ln1_g[i]), w_main, w_dt)
        proj3 = proj.reshape(b, s, -1)

        qn, kn, sel, vt5 = _attn_prep(
            proj3, row(jnp.tile(q_norm_g[i], ATTN_HEADS)), row(jnp.tile(k_norm_g[i], ATTN_HEADS)))
        score_bound = 1.02 * LOG2E * HEAD_DIM ** 0.5 * jnp.max(jnp.abs(q_norm_g[i])) * jnp.max(jnp.abs(k_norm_g[i]))
        bounded = (score_bound <= SAFE_EXP2_RANGE).astype(jnp.int32).reshape(1)
        att = _moba_attention(bounded, qn, kn, sel, vt5)

        yn = _ssd(proj3, dt_raw.reshape(b, s, LANES), _lane_blocks(conv_w[i]), _lane_blocks(row(conv_b[i])),
                  _head_lanes(dt_bias[i]), _head_lanes(a_log[i]), row(jnp.repeat(d_skip[i], HEAD_DIM)),
                  row(ssm_norm_g[i]))

        x1 = _mix_out(att.reshape(b * s, attn_w), yn.reshape(b * s, inner), proj, x2d,
                      w_o_attn[i].astype(BF16), w_o_ssm[i].astype(BF16), w_out[i].astype(BF16))
        x2 = _ffn_ple(x1, row(ln2_g[i]), w_gate_up[i].astype(BF16), w_down[i].astype(BF16),
                      p[i].reshape(b * s, -1), row(ln3_g[i]), w_ple_gate[i].astype(BF16),
                      w_ple_proj[i].astype(BF16))
        x = x2.reshape(b, s, d)
    return x
```

```python
import functools
import math

import jax
import jax.numpy as jnp
from jax import lax
from jax.experimental import pallas as pl
from jax.experimental.pallas import tpu as pltpu

F32 = jnp.float32
BF16 = jnp.bfloat16

EPS = 1e-6
LOG2E = math.log2(math.e)
LANES = 128
HEAD_DIM = 64
ATTN_HEADS = 16
HEAD_PAIRS = ATTN_HEADS // 2
ATTN_PAIRS_PER_STEP = 8
MOBA_BLOCK = 256
MOBA_TOPK = 3
MAX_BLOCKS = 32
SSM_HEADS = 32
SSM_PAIRS = SSM_HEADS // 2
SSM_GROUPS = 4
SSM_STATE = 128
SSM_CONV = 4
SSM_CHUNK = 128
CONV_PAD = 8
NEG = -1e30
SAFE_EXP2_RANGE = 60.0
VMEM_LIMIT = 56 * 1024 * 1024


def _sigmoid(v):
    return 1.0 / (1.0 + jnp.exp(-v))


def _nt_dot(a, b, **kw):
    return lax.dot_general(a, b, (((1,), (1,)), ((), ())), preferred_element_type=F32, **kw)


def _inproj_kernel(n_main, x_ref, g_ref, w_ref, wt_ref, wdt_ref, o_ref, dt_ref, h_ref):
    j = pl.program_id(1)

    @pl.when(j == 0)
    def _():
        x = x_ref[...]
        ms = jnp.mean(x * x, axis=-1, keepdims=True)
        h = (x * lax.rsqrt(ms + EPS) * g_ref[...]).astype(BF16)
        h_ref[...] = h
        dt_ref[...] = jnp.dot(h, wdt_ref[...], preferred_element_type=F32)

    @pl.when(j < n_main)
    def _():
        o_ref[...] = jnp.dot(h_ref[...], w_ref[...].astype(BF16), preferred_element_type=F32).astype(o_ref.dtype)

    @pl.when(j >= n_main)
    def _():
        o_ref[...] = jnp.dot(h_ref[...], wt_ref[...], preferred_element_type=F32).astype(o_ref.dtype)


def _in_proj(x2d, g, w_full, src_blocks, w_tail, w_dt, tm=1024, tn=1024):
    m, d = x2d.shape
    n_main, n_tail = len(src_blocks), w_tail.shape[1] // tn
    n = (n_main + n_tail) * tn

    def main_block(i, j):
        idx = src_blocks[-1]
        for k, src in enumerate(src_blocks[:-1]):
            idx = jnp.where(j == k, src, idx)
        return 0, idx

    return pl.pallas_call(
        functools.partial(_inproj_kernel, n_main),
        grid=(m // tm, n // tn),
        in_specs=[
            pl.BlockSpec((tm, d), lambda i, j: (i, 0)),
            pl.BlockSpec((1, d), lambda i, j: (0, 0)),
            pl.BlockSpec((d, tn), main_block),
            pl.BlockSpec((d, tn), lambda i, j: (0, jnp.where(j >= n_main, j - n_main, n_tail - 1))),
            pl.BlockSpec((d, LANES), lambda i, j: (0, 0)),
        ],
        out_specs=[
            pl.BlockSpec((tm, tn), lambda i, j: (i, j)),
            pl.BlockSpec((tm, LANES), lambda i, j: (i, 0)),
        ],
        out_shape=[jax.ShapeDtypeStruct((m, n), BF16), jax.ShapeDtypeStruct((m, LANES), F32)],
        scratch_shapes=[pltpu.VMEM((tm, d), BF16)],
        compiler_params=pltpu.CompilerParams(
            dimension_semantics=("parallel", "arbitrary"), vmem_limit_bytes=VMEM_LIMIT),
        name="in_proj",
    )(x2d, g, w_full, w_tail, w_dt)


def _head_rms_norm(x, gain, is_a):
    outs = []
    for c in range(x.shape[1] // LANES):
        blk = x[:, c * LANES:(c + 1) * LANES]
        sq = blk * blk
        sa = jnp.sum(jnp.where(is_a, sq, 0.0), axis=-1, keepdims=True)
        sb = jnp.sum(jnp.where(is_a, 0.0, sq), axis=-1, keepdims=True)
        ms = jnp.where(is_a, sa, sb) * (1.0 / HEAD_DIM)
        outs.append(blk * lax.rsqrt(ms + EPS))
    return jnp.concatenate(outs, axis=1) * gain


def _prep_kernel(q_ref, k_ref, v_ref, qg_ref, kg_ref, qn_ref, kn_ref, sel_ref, vt_ref, kmean_ref):
    i = pl.program_id(1)

    @pl.when(i == 0)
    def _():
        kmean_ref[...] = jnp.zeros_like(kmean_ref)

    lane = lax.broadcasted_iota(jnp.int32, (1, LANES), 1)
    is_a = lane < HEAD_DIM
    qn = _head_rms_norm(q_ref[...].astype(F32), qg_ref[...], is_a) * (HEAD_DIM ** -0.5 * LOG2E)
    kn = _head_rms_norm(k_ref[...].astype(F32), kg_ref[...], is_a)
    kn_ref[...] = kn.astype(BF16)

    vt = v_ref[...].astype(F32).T
    vt_ref[...] = vt.reshape(vt_ref.shape).astype(BF16)

    kmean = kmean_ref[...]
    q_hi = qn.astype(BF16)
    qn_ref[...] = q_hi
    q_lo = (qn - q_hi.astype(F32)).astype(BF16)
    km_hi = kmean.astype(BF16)
    km_lo = (kmean - km_hi.astype(F32)).astype(BF16)
    zero = jnp.zeros((), BF16)
    jidx = lax.broadcasted_iota(jnp.int32, (MAX_BLOCKS, MOBA_BLOCK), 0).astype(F32)
    valid = jidx < i.astype(F32)
    for p in range(HEAD_PAIRS):
        cols = slice(p * LANES, (p + 1) * LANES)
        km4 = jnp.concatenate([jnp.where(is_a, km_hi[:, cols], zero), jnp.where(is_a, zero, km_hi[:, cols]),
                               jnp.where(is_a, km_lo[:, cols], zero), jnp.where(is_a, zero, km_lo[:, cols])],
                              axis=0)
        g_hi = _nt_dot(km4, q_hi[:, cols])
        g_lo = _nt_dot(km4[:2 * MAX_BLOCKS], q_lo[:, cols])
        for hh in range(2):
            r0 = hh * MAX_BLOCKS
            gate = (g_hi[r0:r0 + MAX_BLOCKS] + g_hi[r0 + 2 * MAX_BLOCKS:r0 + 3 * MAX_BLOCKS]
                    + g_lo[r0:r0 + MAX_BLOCKS])
            gate = jnp.where(valid, gate, -jnp.inf)
            sel = jnp.zeros_like(gate)
            for _ in range(MOBA_TOPK):
                top = jnp.max(gate, axis=0, keepdims=True)
                first = jnp.min(jnp.where(gate == top, jidx, float(MAX_BLOCKS)), axis=0, keepdims=True)
                pick = jidx == first
                sel = jnp.where(pick, 1.0, sel)
                gate = jnp.where(pick, -jnp.inf, gate)
            sel_ref[2 * p + hh] = jnp.where(valid, sel, 0.0)

    kmean_ref[pl.ds(i, 1), :] = jnp.mean(kn, axis=0, keepdims=True)


def _attn_prep(proj3, qg, kg):
    b, s, _ = proj3.shape
    nb = s // MOBA_BLOCK
    width = ATTN_HEADS * HEAD_DIM
    tile = lambda col: pl.BlockSpec((None, MOBA_BLOCK, width), lambda bb, i, col=col: (bb, i, col))
    gain = pl.BlockSpec((1, width), lambda bb, i: (0, 0))
    return pl.pallas_call(
        _prep_kernel,
        grid=(b, nb),
        in_specs=[tile(0), tile(1), tile(2), gain, gain],
        out_specs=[
            tile(0), tile(0),
            pl.BlockSpec((None, ATTN_HEADS, None, MAX_BLOCKS, MOBA_BLOCK), lambda bb, i: (bb, 0, i, 0, 0)),
            pl.BlockSpec((None, HEAD_PAIRS, None, LANES, MOBA_BLOCK), lambda bb, i: (bb, 0, i, 0, 0)),
        ],
        out_shape=[
            jax.ShapeDtypeStruct((b, s, width), BF16),
            jax.ShapeDtypeStruct((b, s, width), BF16),
            jax.ShapeDtypeStruct((b, ATTN_HEADS, nb, MAX_BLOCKS, MOBA_BLOCK), F32),
            jax.ShapeDtypeStruct((b, HEAD_PAIRS, nb, LANES, MOBA_BLOCK), BF16),
        ],
        scratch_shapes=[pltpu.VMEM((MAX_BLOCKS, width), F32)],
        compiler_params=pltpu.CompilerParams(
            dimension_semantics=("parallel", "arbitrary"), vmem_limit_bytes=VMEM_LIMIT),
        name="attn_prep",
    )(proj3, proj3, proj3, qg, kg)


def _attn_kernel(bounded_ref, q_ref, k_ref, vt_ref, sel_ref, o_ref, qm_ref, acc_ref, s_ref, p_ref):
    i = pl.program_id(2)
    heads = 2 * ATTN_PAIRS_PER_STEP
    lane = lax.broadcasted_iota(jnp.int32, (1, LANES), 1)
    is_a = lane < HEAD_DIM
    zero = jnp.zeros((), BF16)
    for pp in range(ATTN_PAIRS_PER_STEP):
        qp = q_ref[:, pp * LANES:(pp + 1) * LANES]
        qm_ref[2 * pp] = jnp.where(is_a, qp, zero)
        qm_ref[2 * pp + 1] = jnp.where(is_a, zero, qp)

    def scores(j, h):
        pp = h // 2
        kt = k_ref[pl.ds(pl.multiple_of(j * MOBA_BLOCK, MOBA_BLOCK), MOBA_BLOCK), pp * LANES:(pp + 1) * LANES]
        return _nt_dot(kt, qm_ref[h])

    def values(j, h):
        return vt_ref[h // 2, j, (h % 2) * HEAD_DIM:(h % 2 + 1) * HEAD_DIM, :]

    krow = lax.broadcasted_iota(jnp.int32, (MOBA_BLOCK, MOBA_BLOCK), 0)
    qcol = lax.broadcasted_iota(jnp.int32, (MOBA_BLOCK, MOBA_BLOCK), 1)
    causal = krow <= qcol
    zero_row = jnp.zeros((1, MOBA_BLOCK), F32)

    def bounded_scores():
        ls = [zero_row] * heads
        for h in range(heads):
            p = jnp.exp2(jnp.where(causal, scores(i, h), NEG))
            ls[h] = jnp.sum(p, axis=0, keepdims=True)
            p_ref[0, h] = p.astype(BF16)
        for h in range(heads):
            acc_ref[h] = jnp.dot(values(i, h), p_ref[0, h], preferred_element_type=F32)

        def body(jj, ls):
            ls = list(ls)
            for u in range(2):
                j = 2 * jj + u
                for h in range(heads):
                    picked = sel_ref[h, pl.ds(j, 1), :] > 0.0
                    if u == 1:
                        picked = jnp.logical_and(picked, j < i)
                    p = jnp.exp2(scores(j, h) + jnp.where(picked, 0.0, NEG))
                    ls[h] = ls[h] + jnp.sum(p, axis=0, keepdims=True)
                    p_ref[u, h] = p.astype(BF16)
            for u in range(2):
                for h in range(heads):
                    acc_ref[h] += jnp.dot(values(2 * jj + u, h), p_ref[u, h], preferred_element_type=F32)
            return tuple(ls)

        return lax.fori_loop(0, (i + 1) // 2, body, tuple(ls))

    def running_max():
        m0, l0 = [], []
        for h in range(heads):
            s = jnp.where(causal, scores(i, h), NEG)
            m0.append(jnp.max(s, axis=0, keepdims=True))
            s_ref[h] = s
        for h in range(heads):
            p = jnp.exp2(s_ref[h] - m0[h])
            l0.append(jnp.sum(p, axis=0, keepdims=True))
            p_ref[0, h] = p.astype(BF16)
        for h in range(heads):
            acc_ref[h] = jnp.dot(values(i, h), p_ref[0, h], preferred_element_type=F32)

        def body(j, carry):
            ms, ls = carry
            tile_max, new_m, new_l, alphas = [], [], [], []
            for h in range(heads):
                s = scores(j, h)
                tile_max.append(jnp.max(s, axis=0, keepdims=True))
                s_ref[h] = s
            for h in range(heads):
                picked = sel_ref[h, pl.ds(j, 1), :] > 0.0
                m = jnp.maximum(ms[h], jnp.where(picked, tile_max[h], NEG))
                p = jnp.exp2(s_ref[h] - jnp.where(picked, m, -NEG))
                alpha = jnp.exp2(ms[h] - m)
                new_m.append(m)
                new_l.append(alpha * ls[h] + jnp.sum(p, axis=0, keepdims=True))
                alphas.append(alpha)
                p_ref[0, h] = p.astype(BF16)
            for h in range(heads):
                acc_ref[h] = alphas[h] * acc_ref[h] + jnp.dot(values(j, h), p_ref[0, h],
                                                              preferred_element_type=F32)
            return tuple(new_m), tuple(new_l)

        return lax.fori_loop(0, i, body, (tuple(m0), tuple(l0)))[1]

    ls = lax.cond(bounded_ref[0] != 0, bounded_scores, running_max)
    for pp in range(ATTN_PAIRS_PER_STEP):
        out_t = jnp.concatenate([acc_ref[2 * pp] * (1.0 / ls[2 * pp]),
                                 acc_ref[2 * pp + 1] * (1.0 / ls[2 * pp + 1])], axis=0)
        o_ref[:, pp * LANES:(pp + 1) * LANES] = out_t.T.astype(o_ref.dtype)


def _moba_attention(bounded, qn, kn, sel, vt5):
    b, s, width = qn.shape
    nb = s // MOBA_BLOCK
    pps = ATTN_PAIRS_PER_STEP
    gw = pps * LANES
    qtile = pl.BlockSpec((None, MOBA_BLOCK, gw), lambda bb, g, i: (bb, i, g))
    return pl.pallas_call(
        _attn_kernel,
        grid=(b, HEAD_PAIRS // pps, nb),
        in_specs=[
            pl.BlockSpec(memory_space=pltpu.SMEM),
            qtile,
            pl.BlockSpec((None, s, gw), lambda bb, g, i: (bb, 0, g), pipeline_mode=pl.Buffered(1)),
            pl.BlockSpec((None, pps, nb, LANES, MOBA_BLOCK), lambda bb, g, i: (bb, g, 0, 0, 0),
                         pipeline_mode=pl.Buffered(1)),
            pl.BlockSpec((None, 2 * pps, None, MAX_BLOCKS, MOBA_BLOCK), lambda bb, g, i: (bb, g, i, 0, 0)),
        ],
        out_specs=qtile,
        out_shape=jax.ShapeDtypeStruct((b, s, width), BF16),
        scratch_shapes=[
            pltpu.VMEM((2 * pps, MOBA_BLOCK, LANES), BF16),
            pltpu.VMEM((2 * pps, HEAD_DIM, MOBA_BLOCK), F32),
            pltpu.VMEM((2 * pps, MOBA_BLOCK, MOBA_BLOCK), F32),
            pltpu.VMEM((2, 2 * pps, MOBA_BLOCK, MOBA_BLOCK), BF16),
        ],
        compiler_params=pltpu.CompilerParams(
            dimension_semantics=("parallel", "parallel", "arbitrary"), vmem_limit_bytes=VMEM_LIMIT),
        name="moba_attn",
    )(bounded, qn, kn, vt5, sel)


def _ssd_kernel(xbc_ref, z_ref, dt_ref, cw_ref, cb_ref, dtb_ref, alog_ref, dsk_ref, ng_ref, ec_ref, ew_ref,
                o_ref, ext_ref, act_ref, state_ref, y_ref, acst_ref, ldtt_ref, rhs_ref, ly_ref, ls_ref):
    c = pl.program_id(1)
    L = SSM_CHUNK
    n_blk = act_ref.shape[0]
    b_blk = SSM_PAIRS
    c_blk = SSM_PAIRS + SSM_GROUPS

    @pl.when(c == 0)
    def _():
        ext_ref[:, 0:CONV_PAD, :] = jnp.zeros((n_blk, CONV_PAD, LANES), F32)
        state_ref[...] = jnp.zeros_like(state_ref)

    def conv_silu(blk):
        ext_ref[blk, CONV_PAD:CONV_PAD + L, :] = xbc_ref[:, blk * LANES:(blk + 1) * LANES].astype(F32)
        w = cw_ref[blk]
        conv = cb_ref[blk]
        for k in range(SSM_CONV):
            start = CONV_PAD - (SSM_CONV - 1) + k
            conv = conv + w[k:k + 1, :] * ext_ref[blk, start:start + L, :]
        ext_ref[blk, 0:CONV_PAD, :] = ext_ref[blk, L:L + CONV_PAD, :]
        act_ref[blk] = conv * _sigmoid(conv)

    dt_in = dt_ref[...] + dtb_ref[...]
    dtv = jnp.maximum(dt_in, 0.0) + jnp.log(1.0 + jnp.exp(-jnp.abs(dt_in)))
    a2 = dtv * (-LOG2E * jnp.exp(alog_ref[...]))
    row = lax.broadcasted_iota(jnp.int32, (L, L), 0)
    col = lax.broadcasted_iota(jnp.int32, (L, L), 1)
    tri = col <= row
    acs = jnp.dot(tri.astype(F32), a2, preferred_element_type=F32, precision=lax.Precision.HIGHEST)
    acst_ref[...] = acs.T
    ldtt_ref[...] = (jnp.log(dtv) * LOG2E).T
    lane = lax.broadcasted_iota(jnp.int32, (1, LANES), 1)
    is_a = lane < HEAD_DIM
    zero = jnp.zeros((), BF16)

    def split3(v):
        hi = v.astype(BF16)
        r1 = v - hi.astype(F32)
        mid = r1.astype(BF16)
        lo = (r1 - mid.astype(F32)).astype(BF16)
        return jnp.where(lane < SSM_HEADS, hi, jnp.where(lane < 2 * SSM_HEADS, mid, lo))

    acs_parts = split3(acs)
    wst_parts = split3(jnp.exp2(acs))
    ppg = SSM_PAIRS // SSM_GROUPS

    for g in range(SSM_GROUPS):
        for blk in list(range(g * ppg, (g + 1) * ppg)) + [b_blk + g, c_blk + g]:
            conv_silu(blk)
        bg = act_ref[b_blk + g]
        cbm = _nt_dot(act_ref[c_blk + g].astype(BF16), bg.astype(BF16))
        bt = bg.T
        cmat = jnp.dot(acs_parts, ec_ref[g], preferred_element_type=F32)
        wst = jnp.dot(wst_parts, ew_ref[g], preferred_element_type=F32)
        prev = jnp.concatenate([state_ref[g * ppg + pj] for pj in range(ppg)], axis=1).astype(BF16)
        y_off = jnp.dot(act_ref[c_blk + g].astype(BF16), prev, preferred_element_type=F32) * wst
        decs = []
        for pj in range(ppg):
            xp = act_ref[g * ppg + pj].astype(BF16)
            rhs_ref[g * ppg + pj] = jnp.concatenate([jnp.where(is_a, xp, zero), jnp.where(is_a, zero, xp)], axis=0)
            lhs_y, lhs_s, dec = [], [], []
            for hh in range(2):
                k = 2 * pj + hh
                h = g * 2 * ppg + k
                arow = acst_ref[h:h + 1, :]
                rrow = arow - ldtt_ref[h:h + 1, :]
                expo = jnp.where(tri, cmat[:, k * LANES:(k + 1) * LANES] - rrow, NEG)
                lhs_y.append((jnp.exp2(expo) * cbm).astype(BF16))
                tot = arow[:, L - 1:L]
                lhs_s.append((bt * jnp.exp2(tot - rrow)).astype(BF16))
                dec.append(jnp.exp2(tot))
            ly_ref[g * ppg + pj] = jnp.concatenate(lhs_y, axis=1)
            ls_ref[g * ppg + pj] = jnp.concatenate(lhs_s, axis=1)
            decs.append(jnp.where(is_a, dec[0], dec[1]))
        for pj in range(ppg):
            y_ref[g * ppg + pj] = (jnp.dot(ly_ref[g * ppg + pj], rhs_ref[g * ppg + pj], preferred_element_type=F32)
                                   + y_off[:, pj * LANES:(pj + 1) * LANES])
        for pj in range(ppg):
            j = g * ppg + pj
            state_ref[j] = state_ref[j] * decs[pj] + jnp.dot(ls_ref[g * ppg + pj], rhs_ref[g * ppg + pj],
                                                            preferred_element_type=F32)

        ys = []
        ssq = jnp.zeros((L, 1), F32)
        for j in range(g * ppg, (g + 1) * ppg):
            cols = slice(j * LANES, (j + 1) * LANES)
            zf = z_ref[:, cols].astype(F32)
            y = (y_ref[j] + dsk_ref[:, cols] * act_ref[j]) * (zf * _sigmoid(zf))
            ssq = ssq + jnp.sum(y * y, axis=-1, keepdims=True)
            ys.append(y)
        scale = lax.rsqrt(ssq * (1.0 / (ppg * LANES)) + EPS)
        for j, y in zip(range(g * ppg, (g + 1) * ppg), ys):
            cols = slice(j * LANES, (j + 1) * LANES)
            o_ref[:, cols] = (y * scale * ng_ref[:, cols]).astype(o_ref.dtype)


def _head_expansion(width):
    r = lax.broadcasted_iota(jnp.int32, (LANES, SSM_HEADS * width), 0)
    col = lax.broadcasted_iota(jnp.int32, (LANES, SSM_HEADS * width), 1)
    e = jnp.logical_and(r < 3 * SSM_HEADS, r % SSM_HEADS == col // width).astype(BF16)
    return e.reshape(LANES, SSM_GROUPS, -1).transpose(1, 0, 2)


def _ssd(proj3, dt3, conv_w, conv_b, dt_bias, a_log, d_skip_x, norm_g):
    b, s, _ = proj3.shape
    L = SSM_CHUNK
    inner = SSM_HEADS * HEAD_DIM
    conv_dim = inner + 2 * SSM_GROUPS * SSM_STATE
    n_blk = conv_dim // LANES
    hpg = SSM_HEADS // SSM_GROUPS
    ppg = SSM_PAIRS // SSM_GROUPS
    vec = lambda w: pl.BlockSpec((1, w), lambda bb, c: (0, 0))
    return pl.pallas_call(
        _ssd_kernel,
        grid=(b, s // L),
        in_specs=[
            pl.BlockSpec((None, L, conv_dim), lambda bb, c: (bb, c, 1)),
            pl.BlockSpec((None, L, inner), lambda bb, c: (bb, c, 3)),
            pl.BlockSpec((None, L, LANES), lambda bb, c: (bb, c, 0)),
            pl.BlockSpec((n_blk, SSM_CONV, LANES), lambda bb, c: (0, 0, 0)),
            pl.BlockSpec((n_blk, 1, LANES), lambda bb, c: (0, 0, 0)),
            vec(LANES), vec(LANES), vec(inner), vec(inner),
            pl.BlockSpec((SSM_GROUPS, LANES, hpg * L), lambda bb, c: (0, 0, 0)),
            pl.BlockSpec((SSM_GROUPS, LANES, hpg * HEAD_DIM), lambda bb, c: (0, 0, 0)),
        ],
        out_specs=pl.BlockSpec((None, L, inner), lambda bb, c: (bb, c, 0)),
        out_shape=jax.ShapeDtypeStruct((b, s, inner), BF16),
        scratch_shapes=[
            pltpu.VMEM((n_blk, L + CONV_PAD, LANES), F32),
            pltpu.VMEM((n_blk, L, LANES), F32),
            pltpu.VMEM((SSM_PAIRS, SSM_STATE, LANES), F32),
            pltpu.VMEM((SSM_PAIRS, L, LANES), F32),
            pltpu.VMEM((LANES, L), F32),
            pltpu.VMEM((LANES, L), F32),
            pltpu.VMEM((SSM_PAIRS, 2 * L, LANES), BF16),
            pltpu.VMEM((SSM_PAIRS, L, 2 * L), BF16),
            pltpu.VMEM((SSM_PAIRS, SSM_STATE, 2 * L), BF16),
        ],
        compiler_params=pltpu.CompilerParams(
            dimension_semantics=("parallel", "arbitrary"), vmem_limit_bytes=VMEM_LIMIT),
        name="ssd",
    )(proj3, proj3, dt3, conv_w, conv_b, dt_bias, a_log, d_skip_x, norm_g,
      _head_expansion(L), _head_expansion(HEAD_DIM))


def _mix_kernel(att_ref, yn_ref, ga_ref, gb_ref, x_ref, woa_ref, wos_ref, wout_ref, o_ref):
    ya = jnp.dot(att_ref[...], woa_ref[...], preferred_element_type=F32)
    yb = jnp.dot(yn_ref[...], wos_ref[...], preferred_element_type=F32)
    merged = _sigmoid(ga_ref[...].astype(F32)) * ya + _sigmoid(gb_ref[...].astype(F32)) * yb
    o_ref[...] = x_ref[...] + jnp.dot(merged.astype(BF16), wout_ref[...], preferred_element_type=F32)


def _mix_out(att2d, yn2d, proj2d, x2d, w_oa, w_os, w_out, tm=512):
    m, d = x2d.shape
    ga_col = (proj2d.shape[1] - 2 * d) // d
    row = lambda w, col=0: pl.BlockSpec((tm, w), lambda i, col=col: (i, col))
    full = lambda a: pl.BlockSpec(a.shape, lambda i: (0, 0))
    return pl.pallas_call(
        _mix_kernel,
        grid=(m // tm,),
        in_specs=[row(att2d.shape[1]), row(yn2d.shape[1]), row(d, ga_col), row(d, ga_col + 1), row(d),
                  full(w_oa), full(w_os), full(w_out)],
        out_specs=row(d),
        out_shape=jax.ShapeDtypeStruct((m, d), F32),
        compiler_params=pltpu.CompilerParams(
            dimension_semantics=("parallel",), vmem_limit_bytes=VMEM_LIMIT),
        name="mix_out",
    )(att2d, yn2d, proj2d, proj2d, x2d, w_oa, w_os, w_out)


def _rms(x, g):
    return x * lax.rsqrt(jnp.mean(x * x, axis=-1, keepdims=True) + EPS) * g


def _ffn_kernel(x_ref, g2_ref, wg_ref, wu_ref, wd_ref, p_ref, g3_ref, wpg_ref, wpp_ref, o_ref,
                h_ref, acc_ref):
    j = pl.program_id(1)

    @pl.when(j == 0)
    def _():
        h_ref[...] = _rms(x_ref[...], g2_ref[...]).astype(BF16)
        acc_ref[...] = jnp.zeros_like(acc_ref)

    h = h_ref[...]
    gate = jnp.dot(h, wg_ref[...], preferred_element_type=F32)
    up = jnp.dot(h, wu_ref[...], preferred_element_type=F32)
    act = (gate * _sigmoid(gate) * up).astype(BF16)
    acc_ref[...] += jnp.dot(act, wd_ref[...], preferred_element_type=F32)

    @pl.when(j == pl.num_programs(1) - 1)
    def _():
        x2 = x_ref[...] + acc_ref[...]
        h3 = _rms(x2, g3_ref[...]).astype(BF16)
        pgate = _sigmoid(jnp.dot(h3, wpg_ref[...], preferred_element_type=F32))
        proj = jnp.dot(p_ref[...].astype(BF16), wpp_ref[...], preferred_element_type=F32)
        o_ref[...] = x2 + proj * pgate


def _ffn_ple(x2d, g2, w_gu, w_down, p2d, g3, w_pg, w_pp, tm=512, n_ff=2):
    m, d = x2d.shape
    d_ff = w_down.shape[0]
    tf = d_ff // n_ff
    const = lambda a: pl.BlockSpec(a.shape, lambda i, j: (0, 0))
    return pl.pallas_call(
        _ffn_kernel,
        grid=(m // tm, n_ff),
        in_specs=[
            pl.BlockSpec((tm, d), lambda i, j: (i, 0)),
            const(g2),
            pl.BlockSpec((d, tf), lambda i, j: (0, j)),
            pl.BlockSpec((d, tf), lambda i, j: (0, n_ff + j)),
            pl.BlockSpec((tf, d), lambda i, j: (j, 0)),
            pl.BlockSpec((tm, p2d.shape[1]), lambda i, j: (i, 0)),
            const(g3), const(w_pg), const(w_pp),
        ],
        out_specs=pl.BlockSpec((tm, d), lambda i, j: (i, 0)),
        out_shape=jax.ShapeDtypeStruct((m, d), F32),
        scratch_shapes=[pltpu.VMEM((tm, d), BF16), pltpu.VMEM((tm, d), F32)],
        compiler_params=pltpu.CompilerParams(
            dimension_semantics=("parallel", "arbitrary"), vmem_limit_bytes=VMEM_LIMIT),
        name="ffn_ple",
    )(x2d, g2, w_gu, w_gu, w_down, p2d, g3, w_pg, w_pp)


def _head_lanes(v):
    return jnp.pad(jnp.tile(v.astype(F32), 3), (0, LANES - 3 * SSM_HEADS)).reshape(1, LANES)


def _lane_blocks(a):
    r, w = a.shape
    return a.astype(F32).reshape(r, w // LANES, LANES).transpose(1, 0, 2)


def kernel(x, p, ln1_g, w_in, q_norm_g, k_norm_g, w_o_attn, conv_w, conv_b, dt_bias, a_log, d_skip,
           ssm_norm_g, w_o_ssm, w_out, ln2_g, w_gate_up, w_down, ln3_g, w_ple_gate, w_ple_proj):
    b, s, d = x.shape
    assert s % MOBA_BLOCK == 0 and s // MOBA_BLOCK <= MAX_BLOCKS
    attn_w = ATTN_HEADS * HEAD_DIM
    inner = SSM_HEADS * HEAD_DIM
    conv_dim = inner + 2 * SSM_GROUPS * SSM_STATE
    off_z = 3 * attn_w
    off_xbc = off_z + inner
    off_dt = off_xbc + conv_dim
    off_g = off_dt + SSM_HEADS
    row = lambda v: v.astype(F32).reshape(1, -1)

    for i in range(w_in.shape[0]):
        wi = w_in[i]
        tn = 1024
        src_blocks = [t for lo, hi in ((0, off_z), (off_xbc, off_dt), (off_z, off_xbc))
                      for t in range(lo // tn, hi // tn)]
        w_tail = wi[:, off_g:].astype(BF16)
        w_dt =jnp.pad(jnp.tile(wi[:, off_dt:off_g], (1, 3)), ((0, 0), (0, LANES - 3 * SSM_HEADS))).astype(BF16)

        x2d = x.reshape(b * s, d)
        proj, dt_raw = _in_proj(x2d, row(ln1_g[i]), wi, src_blocks, w_tail, w_dt, tn=tn)
        proj3 = proj.reshape(b, s, -1)

        qn, kn, sel, vt5 = _attn_prep(
            proj3, row(jnp.tile(q_norm_g[i], ATTN_HEADS)), row(jnp.tile(k_norm_g[i], ATTN_HEADS)))
        score_bound = 1.02 * LOG2E * HEAD_DIM ** 0.5 * jnp.max(jnp.abs(q_norm_g[i])) * jnp.max(jnp.abs(k_norm_g[i]))
        bounded = (score_bound <= SAFE_EXP2_RANGE).astype(jnp.int32).reshape(1)
        att = _moba_attention(bounded, qn, kn, sel, vt5)

        yn = _ssd(proj3, dt_raw.reshape(b, s, LANES), _lane_blocks(conv_w[i]), _lane_blocks(row(conv_b[i])),
                  _head_lanes(dt_bias[i]), _head_lanes(a_log[i]), row(jnp.repeat(d_skip[i], HEAD_DIM)),
                  row(ssm_norm_g[i]))

        x1 = _mix_out(att.reshape(b * s, attn_w), yn.reshape(b * s, inner), proj, x2d,
                      w_o_attn[i].astype(BF16), w_o_ssm[i].astype(BF16), w_out[i].astype(BF16))
        x2 = _ffn_ple(x1, row(ln2_g[i]), w_gate_up[i].astype(BF16), w_down[i].astype(BF16),
                      p[i].reshape(b * s, -1), row(ln3_g[i]), w_ple_gate[i].astype(BF16),
                      w_ple_proj[i].astype(BF16))
        x = x2.reshape(b, s, d)
    return x
```

```python
import math

import jax
import jax.numpy as jnp
from jax import lax
from jax.experimental import pallas as pl
from jax.experimental.pallas import tpu as pltpu

F32 = jnp.float32
BF16 = jnp.bfloat16

EPS = 1e-6
LOG2E = math.log2(math.e)
LANES = 128
HEAD_DIM = 64
ATTN_HEADS = 16
HEAD_PAIRS = ATTN_HEADS // 2
ATTN_PAIRS_PER_STEP = 8
MOBA_BLOCK = 256
MOBA_TOPK = 3
MAX_BLOCKS = 32
SSM_HEADS = 32
SSM_PAIRS = SSM_HEADS // 2
SSM_GROUPS = 4
SSM_STATE = 128
SSM_CONV = 4
SSM_CHUNK = 128
CONV_PAD = 8
NEG = -1e30
SAFE_EXP2_RANGE = 60.0
VMEM_LIMIT = 56 * 1024 * 1024


def _sigmoid(v):
    return 1.0 / (1.0 + jnp.exp(-v))


def _nt_dot(a, b, **kw):
    return lax.dot_general(a, b, (((1,), (1,)), ((), ())), preferred_element_type=F32, **kw)


def _inproj_kernel(x_ref, g_ref, w_ref, wdt_ref, o_ref, dt_ref, h_ref):
    @pl.when(pl.program_id(1) == 0)
    def _():
        x = x_ref[...]
        ms = jnp.mean(x * x, axis=-1, keepdims=True)
        h = (x * lax.rsqrt(ms + EPS) * g_ref[...]).astype(BF16)
        h_ref[...] = h
        dt_ref[...] = jnp.dot(h, wdt_ref[...], preferred_element_type=F32)

    o_ref[...] = _nt_dot(h_ref[...], w_ref[...]).astype(o_ref.dtype)


def _in_proj(x2d, g, w_main, w_dt, tm=2048, tn=1024):
    m, d = x2d.shape
    n = w_main.shape[0]
    return pl.pallas_call(
        _inproj_kernel,
        grid=(m // tm, n // tn),
        in_specs=[
            pl.BlockSpec((tm, d), lambda i, j: (i, 0)),
            pl.BlockSpec((1, d), lambda i, j: (0, 0)),
            pl.BlockSpec((tn, d), lambda i, j: (j, 0)),
            pl.BlockSpec((d, LANES), lambda i, j: (0, 0)),
        ],
        out_specs=[
            pl.BlockSpec((tm, tn), lambda i, j: (i, j)),
            pl.BlockSpec((tm, LANES), lambda i, j: (i, 0)),
        ],
        out_shape=[jax.ShapeDtypeStruct((m, n), BF16), jax.ShapeDtypeStruct((m, LANES), F32)],
        scratch_shapes=[pltpu.VMEM((tm, d), BF16)],
        compiler_params=pltpu.CompilerParams(
            dimension_semantics=("parallel", "arbitrary"), vmem_limit_bytes=VMEM_LIMIT),
        name="in_proj",
    )(x2d, g, w_main, w_dt)


def _head_rms_norm(x, gain, is_a):
    outs = []
    for c in range(x.shape[1] // LANES):
        blk = x[:, c * LANES:(c + 1) * LANES]
        sq = blk * blk
        sa = jnp.sum(jnp.where(is_a, sq, 0.0), axis=-1, keepdims=True)
        sb = jnp.sum(jnp.where(is_a, 0.0, sq), axis=-1, keepdims=True)
        ms = jnp.where(is_a, sa, sb) * (1.0 / HEAD_DIM)
        outs.append(blk * lax.rsqrt(ms + EPS))
    return jnp.concatenate(outs, axis=1) * gain


def _prep_kernel(q_ref, k_ref, v_ref, qg_ref, kg_ref, qn_ref, kn_ref, sel_ref, vt_ref, kmean_ref):
    i = pl.program_id(1)

    @pl.when(i == 0)
    def _():
        kmean_ref[...] = jnp.zeros_like(kmean_ref)

    lane = lax.broadcasted_iota(jnp.int32, (1, LANES), 1)
    is_a = lane < HEAD_DIM
    qn = _head_rms_norm(q_ref[...].astype(F32), qg_ref[...], is_a) * (HEAD_DIM ** -0.5 * LOG2E)
    kn = _head_rms_norm(k_ref[...].astype(F32), kg_ref[...], is_a)
    kn_ref[...] = kn.astype(BF16)

    vt = v_ref[...].astype(F32).T
    vt_ref[...] = vt.reshape(vt_ref.shape).astype(BF16)

    kmean = kmean_ref[...]
    q_hi = qn.astype(BF16)
    qn_ref[...] = q_hi
    q_lo = (qn - q_hi.astype(F32)).astype(BF16)
    km_hi = kmean.astype(BF16)
    km_lo = (kmean - km_hi.astype(F32)).astype(BF16)
    zero = jnp.zeros((), BF16)
    jidx = lax.broadcasted_iota(jnp.int32, (MAX_BLOCKS, MOBA_BLOCK), 0).astype(F32)
    valid = jidx < i.astype(F32)
    for p in range(HEAD_PAIRS):
        cols = slice(p * LANES, (p + 1) * LANES)
        km4 = jnp.concatenate([jnp.where(is_a, km_hi[:, cols], zero), jnp.where(is_a, zero, km_hi[:, cols]),
                               jnp.where(is_a, km_lo[:, cols], zero), jnp.where(is_a, zero, km_lo[:, cols])],
                              axis=0)
        g_hi = _nt_dot(km4, q_hi[:, cols])
        g_lo = _nt_dot(km4[:2 * MAX_BLOCKS], q_lo[:, cols])
        for hh in range(2):
            r0 = hh * MAX_BLOCKS
            gate = (g_hi[r0:r0 + MAX_BLOCKS] + g_hi[r0 + 2 * MAX_BLOCKS:r0 + 3 * MAX_BLOCKS]
                    + g_lo[r0:r0 + MAX_BLOCKS])
            gate = jnp.where(valid, gate, -jnp.inf)
            sel = jnp.zeros_like(gate)
            for _ in range(MOBA_TOPK):
                top = jnp.max(gate, axis=0, keepdims=True)
                first = jnp.min(jnp.where(gate == top, jidx, float(MAX_BLOCKS)), axis=0, keepdims=True)
                pick = jidx == first
                sel = jnp.where(pick, 1.0, sel)
                gate = jnp.where(pick, -jnp.inf, gate)
            sel_ref[2 * p + hh] = jnp.where(valid, sel, 0.0)

    kmean_ref[pl.ds(i, 1), :] = jnp.mean(kn, axis=0, keepdims=True)


def _attn_prep(proj3, qg, kg):
    b, s, _ = proj3.shape
    nb = s // MOBA_BLOCK
    width = ATTN_HEADS * HEAD_DIM
    tile = lambda col: pl.BlockSpec((None, MOBA_BLOCK, width), lambda bb, i, col=col: (bb, i, col))
    gain = pl.BlockSpec((1, width), lambda bb, i: (0, 0))
    return pl.pallas_call(
        _prep_kernel,
        grid=(b, nb),
        in_specs=[tile(0), tile(1), tile(2), gain, gain],
        out_specs=[
            tile(0), tile(0),
            pl.BlockSpec((None, ATTN_HEADS, None, MAX_BLOCKS, MOBA_BLOCK), lambda bb, i: (bb, 0, i, 0, 0)),
            pl.BlockSpec((None, HEAD_PAIRS, None, LANES, MOBA_BLOCK), lambda bb, i: (bb, 0, i, 0, 0)),
        ],
        out_shape=[
            jax.ShapeDtypeStruct((b, s, width), BF16),
            jax.ShapeDtypeStruct((b, s, width), BF16),
            jax.ShapeDtypeStruct((b, ATTN_HEADS, nb, MAX_BLOCKS, MOBA_BLOCK), F32),
            jax.ShapeDtypeStruct((b, HEAD_PAIRS, nb, LANES, MOBA_BLOCK), BF16),
        ],
        scratch_shapes=[pltpu.VMEM((MAX_BLOCKS, width), F32)],
        compiler_params=pltpu.CompilerParams(
            dimension_semantics=("parallel", "arbitrary"), vmem_limit_bytes=VMEM_LIMIT),
        name="attn_prep",
    )(proj3, proj3, proj3, qg, kg)


def _attn_kernel(bounded_ref, q_ref, k_ref, vt_ref, sel_ref, o_ref, qm_ref, acc_ref, s_ref, p_ref):
    i = pl.program_id(2)
    heads = 2 * ATTN_PAIRS_PER_STEP
    lane = lax.broadcasted_iota(jnp.int32, (1, LANES), 1)
    is_a = lane < HEAD_DIM
    zero = jnp.zeros((), BF16)
    for pp in range(ATTN_PAIRS_PER_STEP):
        qp = q_ref[:, pp * LANES:(pp + 1) * LANES]
        qm_ref[2 * pp] = jnp.where(is_a, qp, zero)
        qm_ref[2 * pp + 1] = jnp.where(is_a, zero, qp)

    def scores(j, h):
        pp = h // 2
        kt = k_ref[pl.ds(pl.multiple_of(j * MOBA_BLOCK, MOBA_BLOCK), MOBA_BLOCK), pp * LANES:(pp + 1) * LANES]
        return _nt_dot(kt, qm_ref[h])

    def values(j, h):
        return vt_ref[h // 2, j, (h % 2) * HEAD_DIM:(h % 2 + 1) * HEAD_DIM, :]

    krow = lax.broadcasted_iota(jnp.int32, (MOBA_BLOCK, MOBA_BLOCK), 0)
    qcol = lax.broadcasted_iota(jnp.int32, (MOBA_BLOCK, MOBA_BLOCK), 1)
    causal = krow <= qcol
    zero_row = jnp.zeros((1, MOBA_BLOCK), F32)

    def bounded_scores():
        ls = [zero_row] * heads
        for h in range(heads):
            p = jnp.exp2(jnp.where(causal, scores(i, h), NEG))
            ls[h] = jnp.sum(p, axis=0, keepdims=True)
            p_ref[0, h] = p.astype(BF16)
        for h in range(heads):
            acc_ref[h] = jnp.dot(values(i, h), p_ref[0, h], preferred_element_type=F32)

        def body(jj, ls):
            ls = list(ls)
            for u in range(2):
                j = 2 * jj + u
                for h in range(heads):
                    picked = sel_ref[h, pl.ds(j, 1), :] > 0.0
                    if u == 1:
                        picked = jnp.logical_and(picked, j < i)
                    p = jnp.exp2(scores(j, h) + jnp.where(picked, 0.0, NEG))
                    ls[h] = ls[h] + jnp.sum(p, axis=0, keepdims=True)
                    p_ref[u, h] = p.astype(BF16)
            for u in range(2):
                for h in range(heads):
                    acc_ref[h] += jnp.dot(values(2 * jj + u, h), p_ref[u, h], preferred_element_type=F32)
            return tuple(ls)

        return lax.fori_loop(0, (i + 1) // 2, body, tuple(ls))

    def running_max():
        m0, l0 = [], []
        for h in range(heads):
            s = jnp.where(causal, scores(i, h), NEG)
            m0.append(jnp.max(s, axis=0, keepdims=True))
            s_ref[h] = s
        for h in range(heads):
            p = jnp.exp2(s_ref[h] - m0[h])
            l0.append(jnp.sum(p, axis=0, keepdims=True))
            p_ref[0, h] = p.astype(BF16)
        for h in range(heads):
            acc_ref[h] = jnp.dot(values(i, h), p_ref[0, h], preferred_element_type=F32)

        def body(j, carry):
            ms, ls = carry
            tile_max, new_m, new_l, alphas = [], [], [], []
            for h in range(heads):
                s = scores(j, h)
                tile_max.append(jnp.max(s, axis=0, keepdims=True))
                s_ref[h] = s
            for h in range(heads):
                picked = sel_ref[h, pl.ds(j, 1), :] > 0.0
                m = jnp.maximum(ms[h], jnp.where(picked, tile_max[h], NEG))
                p = jnp.exp2(s_ref[h] - jnp.where(picked, m, -NEG))
                alpha = jnp.exp2(ms[h] - m)
                new_m.append(m)
                new_l.append(alpha * ls[h] + jnp.sum(p, axis=0, keepdims=True))
                alphas.append(alpha)
                p_ref[0, h] = p.astype(BF16)
            for h in range(heads):
                acc_ref[h] = alphas[h] * acc_ref[h] + jnp.dot(values(j, h), p_ref[0, h],
                                                              preferred_element_type=F32)
            return tuple(new_m), tuple(new_l)

        return lax.fori_loop(0, i, body, (tuple(m0), tuple(l0)))[1]

    ls = lax.cond(bounded_ref[0] != 0, bounded_scores, running_max)
    for pp in range(ATTN_PAIRS_PER_STEP):
        out_t = jnp.concatenate([acc_ref[2 * pp] * (1.0 / ls[2 * pp]),
                                 acc_ref[2 * pp + 1] * (1.0 / ls[2 * pp + 1])], axis=0)
        o_ref[:, pp * LANES:(pp + 1) * LANES] = out_t.T.astype(o_ref.dtype)


def _moba_attention(bounded, qn, kn, sel, vt5):
    b, s, width = qn.shape
    nb = s // MOBA_BLOCK
    pps = ATTN_PAIRS_PER_STEP
    gw = pps * LANES
    qtile = pl.BlockSpec((None, MOBA_BLOCK, gw), lambda bb, g, i: (bb, i, g))
    return pl.pallas_call(
        _attn_kernel,
        grid=(b, HEAD_PAIRS // pps, nb),
        in_specs=[
            pl.BlockSpec(memory_space=pltpu.SMEM),
            qtile,
            pl.BlockSpec((None, s, gw), lambda bb, g, i: (bb, 0, g), pipeline_mode=pl.Buffered(1)),
            pl.BlockSpec((None, pps, nb, LANES, MOBA_BLOCK), lambda bb, g, i: (bb, g, 0, 0, 0),
                         pipeline_mode=pl.Buffered(1)),
            pl.BlockSpec((None, 2 * pps, None, MAX_BLOCKS, MOBA_BLOCK), lambda bb, g, i: (bb, g, i, 0, 0)),
        ],
        out_specs=qtile,
        out_shape=jax.ShapeDtypeStruct((b, s, width), BF16),
        scratch_shapes=[
            pltpu.VMEM((2 * pps, MOBA_BLOCK, LANES), BF16),
            pltpu.VMEM((2 * pps, HEAD_DIM, MOBA_BLOCK), F32),
            pltpu.VMEM((2 * pps, MOBA_BLOCK, MOBA_BLOCK), F32),
            pltpu.VMEM((2, 2 * pps, MOBA_BLOCK, MOBA_BLOCK), BF16),
        ],
        compiler_params=pltpu.CompilerParams(
            dimension_semantics=("parallel", "parallel", "arbitrary"), vmem_limit_bytes=VMEM_LIMIT),
        name="moba_attn",
    )(bounded, qn, kn, vt5, sel)


def _ssd_kernel(xbc_ref, z_ref, dt_ref, cw_ref, cb_ref, dtb_ref, alog_ref, dsk_ref, ng_ref, ec_ref, ew_ref,
                o_ref, ext_ref, act_ref, state_ref, y_ref, acst_ref, ldtt_ref, rhs_ref, ly_ref, ls_ref):
    c = pl.program_id(1)
    L = SSM_CHUNK
    n_blk = act_ref.shape[0]
    b_blk = SSM_PAIRS
    c_blk = SSM_PAIRS + SSM_GROUPS

    @pl.when(c == 0)
    def _():
        ext_ref[:, 0:CONV_PAD, :] = jnp.zeros((n_blk, CONV_PAD, LANES), F32)
        state_ref[...] = jnp.zeros_like(state_ref)

    def conv_silu(blk):
        ext_ref[blk, CONV_PAD:CONV_PAD + L, :] = xbc_ref[:, blk * LANES:(blk + 1) * LANES].astype(F32)
        w = cw_ref[blk]
        conv = cb_ref[blk]
        for k in range(SSM_CONV):
            start = CONV_PAD - (SSM_CONV - 1) + k
            conv = conv + w[k:k + 1, :] * ext_ref[blk, start:start + L, :]
        ext_ref[blk, 0:CONV_PAD, :] = ext_ref[blk, L:L + CONV_PAD, :]
        act_ref[blk] = conv * _sigmoid(conv)

    dt_in = dt_ref[...] + dtb_ref[...]
    dtv = jnp.maximum(dt_in, 0.0) + jnp.log(1.0 + jnp.exp(-jnp.abs(dt_in)))
    a2 = dtv * (-LOG2E * jnp.exp(alog_ref[...]))
    row = lax.broadcasted_iota(jnp.int32, (L, L), 0)
    col = lax.broadcasted_iota(jnp.int32, (L, L), 1)
    tri = col <= row
    acs = jnp.dot(tri.astype(F32), a2, preferred_element_type=F32, precision=lax.Precision.HIGHEST)
    acst_ref[...] = acs.T
    ldtt_ref[...] = (jnp.log(dtv) * LOG2E).T
    lane = lax.broadcasted_iota(jnp.int32, (1, LANES), 1)
    is_a = lane < HEAD_DIM
    zero = jnp.zeros((), BF16)

    def split3(v):
        hi = v.astype(BF16)
        r1 = v - hi.astype(F32)
        mid = r1.astype(BF16)
        lo = (r1 - mid.astype(F32)).astype(BF16)
        return jnp.where(lane < SSM_HEADS, hi, jnp.where(lane < 2 * SSM_HEADS, mid, lo))

    acs_parts = split3(acs)
    wst_parts = split3(jnp.exp2(acs))
    ppg = SSM_PAIRS // SSM_GROUPS

    for g in range(SSM_GROUPS):
        for blk in list(range(g * ppg, (g + 1) * ppg)) + [b_blk + g, c_blk + g]:
            conv_silu(blk)
        bg = act_ref[b_blk + g]
        cbm = _nt_dot(act_ref[c_blk + g].astype(BF16), bg.astype(BF16))
        bt = bg.T
        cmat = jnp.dot(acs_parts, ec_ref[g], preferred_element_type=F32)
        wst = jnp.dot(wst_parts, ew_ref[g], preferred_element_type=F32)
        prev = jnp.concatenate([state_ref[g * ppg + pj] for pj in range(ppg)], axis=1).astype(BF16)
        y_off = jnp.dot(act_ref[c_blk + g].astype(BF16), prev, preferred_element_type=F32) * wst
        decs = []
        for pj in range(ppg):
            xp = act_ref[g * ppg + pj].astype(BF16)
            rhs_ref[g * ppg + pj] = jnp.concatenate([jnp.where(is_a, xp, zero), jnp.where(is_a, zero, xp)], axis=0)
            lhs_y, lhs_s, dec = [], [], []
            for hh in range(2):
                k = 2 * pj + hh
                h = g * 2 * ppg + k
                arow = acst_ref[h:h + 1, :]
                rrow = arow - ldtt_ref[h:h + 1, :]
                expo = jnp.where(tri, cmat[:, k * LANES:(k + 1) * LANES] - rrow, NEG)
                lhs_y.append((jnp.exp2(expo) * cbm).astype(BF16))
                tot = arow[:, L - 1:L]
                lhs_s.append((bt * jnp.exp2(tot - rrow)).astype(BF16))
                dec.append(jnp.exp2(tot))
            ly_ref[g * ppg + pj] = jnp.concatenate(lhs_y, axis=1)
            ls_ref[g * ppg + pj] = jnp.concatenate(lhs_s, axis=1)
            decs.append(jnp.where(is_a, dec[0], dec[1]))
        for pj in range(ppg):
            y_ref[g * ppg + pj] = (jnp.dot(ly_ref[g * ppg + pj], rhs_ref[g * ppg + pj], preferred_element_type=F32)
                                   + y_off[:, pj * LANES:(pj + 1) * LANES])
        for pj in range(ppg):
            j = g * ppg + pj
            state_ref[j] = state_ref[j] * decs[pj] + jnp.dot(ls_ref[g * ppg + pj], rhs_ref[g * ppg + pj],
                                                            preferred_element_type=F32)

        ys = []
        ssq = jnp.zeros((L, 1), F32)
        for j in range(g * ppg, (g + 1) * ppg):
            cols = slice(j * LANES, (j + 1) * LANES)
            zf = z_ref[:, cols].astype(F32)
            y = (y_ref[j] + dsk_ref[:, cols] * act_ref[j]) * (zf * _sigmoid(zf))
            ssq = ssq + jnp.sum(y * y, axis=-1, keepdims=True)
            ys.append(y)
        scale = lax.rsqrt(ssq * (1.0 / (ppg * LANES)) + EPS)
        for j, y in zip(range(g * ppg, (g + 1) * ppg), ys):
            cols = slice(j * LANES, (j + 1) * LANES)
            o_ref[:, cols] = (y * scale * ng_ref[:, cols]).astype(o_ref.dtype)


def _head_expansion(width):
    r = lax.broadcasted_iota(jnp.int32, (LANES, SSM_HEADS * width), 0)
    col = lax.broadcasted_iota(jnp.int32, (LANES, SSM_HEADS * width), 1)
    e = jnp.logical_and(r < 3 * SSM_HEADS, r % SSM_HEADS == col // width).astype(BF16)
    return e.reshape(LANES, SSM_GROUPS, -1).transpose(1, 0, 2)


def _ssd(proj3, dt3, conv_w, conv_b, dt_bias, a_log, d_skip_x, norm_g):
    b, s, _ = proj3.shape
    L = SSM_CHUNK
    inner = SSM_HEADS * HEAD_DIM
    conv_dim = inner + 2 * SSM_GROUPS * SSM_STATE
    n_blk = conv_dim // LANES
    hpg = SSM_HEADS // SSM_GROUPS
    ppg = SSM_PAIRS // SSM_GROUPS
    vec = lambda w: pl.BlockSpec((1, w), lambda bb, c: (0, 0))
    return pl.pallas_call(
        _ssd_kernel,
        grid=(b, s // L),
        in_specs=[
            pl.BlockSpec((None, L, conv_dim), lambda bb, c: (bb, c, 1)),
            pl.BlockSpec((None, L, inner), lambda bb, c: (bb, c, 3)),
            pl.BlockSpec((None, L, LANES), lambda bb, c: (bb, c, 0)),
            pl.BlockSpec((n_blk, SSM_CONV, LANES), lambda bb, c: (0, 0, 0)),
            pl.BlockSpec((n_blk, 1, LANES), lambda bb, c: (0, 0, 0)),
            vec(LANES), vec(LANES), vec(inner), vec(inner),
            pl.BlockSpec((SSM_GROUPS, LANES, hpg * L), lambda bb, c: (0, 0, 0)),
            pl.BlockSpec((SSM_GROUPS, LANES, hpg * HEAD_DIM), lambda bb, c: (0, 0, 0)),
        ],
        out_specs=pl.BlockSpec((None, L, inner), lambda bb, c: (bb, c, 0)),
        out_shape=jax.ShapeDtypeStruct((b, s, inner), BF16),
        scratch_shapes=[
            pltpu.VMEM((n_blk, L + CONV_PAD, LANES), F32),
            pltpu.VMEM((n_blk, L, LANES), F32),
            pltpu.VMEM((SSM_PAIRS, SSM_STATE, LANES), F32),
            pltpu.VMEM((SSM_PAIRS, L, LANES), F32),
            pltpu.VMEM((LANES, L), F32),
            pltpu.VMEM((LANES, L), F32),
            pltpu.VMEM((SSM_PAIRS, 2 * L, LANES), BF16),
            pltpu.VMEM((SSM_PAIRS, L, 2 * L), BF16),
            pltpu.VMEM((SSM_PAIRS, SSM_STATE, 2 * L), BF16),
        ],
        compiler_params=pltpu.CompilerParams(
            dimension_semantics=("parallel", "arbitrary"), vmem_limit_bytes=VMEM_LIMIT),
        name="ssd",
    )(proj3, proj3, dt3, conv_w, conv_b, dt_bias, a_log, d_skip_x, norm_g,
      _head_expansion(L), _head_expansion(HEAD_DIM))


def _mix_kernel(att_ref, yn_ref, ga_ref, gb_ref, x_ref, woa_ref, wos_ref, wout_ref, o_ref):
    ya = jnp.dot(att_ref[...], woa_ref[...], preferred_element_type=F32)
    yb = jnp.dot(yn_ref[...], wos_ref[...], preferred_element_type=F32)
    merged = _sigmoid(ga_ref[...].astype(F32)) * ya + _sigmoid(gb_ref[...].astype(F32)) * yb
    o_ref[...] = x_ref[...] + jnp.dot(merged.astype(BF16), wout_ref[...], preferred_element_type=F32)


def _mix_out(att2d, yn2d, proj2d, x2d, w_oa, w_os, w_out, tm=512):
    m, d = x2d.shape
    ga_col = (proj2d.shape[1] - 2 * d) // d
    row = lambda w, col=0: pl.BlockSpec((tm, w), lambda i, col=col: (i, col))
    full = lambda a: pl.BlockSpec(a.shape, lambda i: (0, 0))
    return pl.pallas_call(
        _mix_kernel,
        grid=(m // tm,),
        in_specs=[row(att2d.shape[1]), row(yn2d.shape[1]), row(d, ga_col), row(d, ga_col + 1), row(d),
                  full(w_oa), full(w_os), full(w_out)],
        out_specs=row(d),
        out_shape=jax.ShapeDtypeStruct((m, d), F32),
        compiler_params=pltpu.CompilerParams(
            dimension_semantics=("parallel",), vmem_limit_bytes=VMEM_LIMIT),
        name="mix_out",
    )(att2d, yn2d, proj2d, proj2d, x2d, w_oa, w_os, w_out)


def _rms(x, g):
    return x * lax.rsqrt(jnp.mean(x * x, axis=-1, keepdims=True) + EPS) * g


def _ffn_kernel(x_ref, g2_ref, wg_ref, wu_ref, wd_ref, p_ref, g3_ref, wpg_ref, wpp_ref, o_ref,
                h_ref, acc_ref):
    j = pl.program_id(1)

    @pl.when(j == 0)
    def _():
        h_ref[...] = _rms(x_ref[...], g2_ref[...]).astype(BF16)
        acc_ref[...] = jnp.zeros_like(acc_ref)

    h = h_ref[...]
    gate = jnp.dot(h, wg_ref[...], preferred_element_type=F32)
    up = jnp.dot(h, wu_ref[...], preferred_element_type=F32)
    act = (gate * _sigmoid(gate) * up).astype(BF16)
    acc_ref[...] += jnp.dot(act, wd_ref[...], preferred_element_type=F32)

    @pl.when(j == pl.num_programs(1) - 1)
    def _():
        x2 = x_ref[...] + acc_ref[...]
        h3 = _rms(x2, g3_ref[...]).astype(BF16)
        pgate = _sigmoid(jnp.dot(h3, wpg_ref[...], preferred_element_type=F32))
        proj = jnp.dot(p_ref[...].astype(BF16), wpp_ref[...], preferred_element_type=F32)
        o_ref[...] = x2 + proj * pgate


def _ffn_ple(x2d, g2, w_gu, w_down, p2d, g3, w_pg, w_pp, tm=512, n_ff=2):
    m, d = x2d.shape
    d_ff = w_down.shape[0]
    tf = d_ff // n_ff
    const = lambda a: pl.BlockSpec(a.shape, lambda i, j: (0, 0))
    return pl.pallas_call(
        _ffn_kernel,
        grid=(m // tm, n_ff),
        in_specs=[
            pl.BlockSpec((tm, d), lambda i, j: (i, 0)),
            const(g2),
            pl.BlockSpec((d, tf), lambda i, j: (0, j)),
            pl.BlockSpec((d, tf), lambda i, j: (0, n_ff + j)),
            pl.BlockSpec((tf, d), lambda i, j: (j, 0)),
            pl.BlockSpec((tm, p2d.shape[1]), lambda i, j: (i, 0)),
            const(g3), const(w_pg), const(w_pp),
        ],
        out_specs=pl.BlockSpec((tm, d), lambda i, j: (i, 0)),
        out_shape=jax.ShapeDtypeStruct((m, d), F32),
        scratch_shapes=[pltpu.VMEM((tm, d), BF16), pltpu.VMEM((tm, d), F32)],
        compiler_params=pltpu.CompilerParams(
            dimension_semantics=("parallel", "arbitrary"), vmem_limit_bytes=VMEM_LIMIT),
        name="ffn_ple",
    )(x2d, g2, w_gu, w_gu, w_down, p2d, g3, w_pg, w_pp)


def _head_lanes(v):
    return jnp.pad(jnp.tile(v.astype(F32), 3), (0, LANES - 3 * SSM_HEADS)).reshape(1, LANES)


def _lane_blocks(a):
    r, w = a.shape
    return a.astype(F32).reshape(r, w // LANES, LANES).transpose(1, 0, 2)


def kernel(x, p, ln1_g, w_in, q_norm_g, k_norm_g, w_o_attn, conv_w, conv_b, dt_bias, a_log, d_skip,
           ssm_norm_g, w_o_ssm, w_out, ln2_g, w_gate_up, w_down, ln3_g, w_ple_gate, w_ple_proj):
    b, s, d = x.shape
    assert s % MOBA_BLOCK == 0 and s // MOBA_BLOCK <= MAX_BLOCKS
    attn_w = ATTN_HEADS * HEAD_DIM
    inner = SSM_HEADS * HEAD_DIM
    conv_dim = inner + 2 * SSM_GROUPS * SSM_STATE
    off_z = 3 * attn_w
    off_xbc = off_z + inner
    off_dt = off_xbc + conv_dim
    off_g = off_dt + SSM_HEADS
    row = lambda v: v.astype(F32).reshape(1, -1)

    for i in range(w_in.shape[0]):
        wi = w_in[i]
        wt = wi.T
        w_main = jnp.concatenate(
            [wt[:off_z], wt[off_xbc:off_dt], wt[off_z:off_xbc], wt[off_g:]], axis=0).astype(BF16)
        w_dt = jnp.pad(jnp.tile(wi[:, off_dt:off_g], (1, 3)), ((0, 0), (0, LANES - 3 * SSM_HEADS))).astype(BF16)

        x2d = x.reshape(b * s, d)
        proj, dt_raw = _in_proj(x2d, row(ln1_g[i]), w_main, w_dt)
        proj3 = proj.reshape(b, s, -1)

        qn, kn, sel, vt5 = _attn_prep(
            proj3, row(jnp.tile(q_norm_g[i], ATTN_HEADS)), row(jnp.tile(k_norm_g[i], ATTN_HEADS)))
        score_bound = 1.02 * LOG2E * HEAD_DIM ** 0.5 * jnp.max(jnp.abs(q_norm_g[i])) * jnp.max(jnp.abs(k_norm_g[i]))
        bounded = (score_bound <= SAFE_EXP2_RANGE).astype(jnp.int32).reshape(1)
        att = _moba_attention(bounded, qn, kn, sel, vt5)

        yn = _ssd(proj3, dt_raw.reshape(b, s, LANES), _lane_blocks(conv_w[i]), _lane_blocks(row(conv_b[i])),
                  _head_lanes(dt_bias[i]), _head_lanes(a_log[i]), row(jnp.repeat(d_skip[i], HEAD_DIM)),
                  row(ssm_norm_g[i]))

        x1 = _mix_out(att.reshape(b * s, attn_w), yn.reshape(b * s, inner), proj, x2d,
                      w_o_attn[i].astype(BF16), w_o_ssm[i].astype(BF16), w_out[i].astype(BF16))
        x2 = _ffn_ple(x1, row(ln2_g[i]), w_gate_up[i].astype(BF16), w_down[i].astype(BF16),
                      p[i].reshape(b * s, -1), row(ln3_g[i]), w_ple_gate[i].astype(BF16),
                      w_ple_proj[i].astype(BF16))
        x = x2.reshape(b, s, d)
    return x
```

```python
import math

import jax
import jax.numpy as jnp
from jax import lax
from jax.experimental import pallas as pl
from jax.experimental.pallas import tpu as pltpu

F32 = jnp.float32
BF16 = jnp.bfloat16

EPS = 1e-6
LOG2E = math.log2(math.e)
LANES = 128
HEAD_DIM = 64
ATTN_HEADS = 16
HEAD_PAIRS = ATTN_HEADS // 2
ATTN_PAIRS_PER_STEP = 8
MOBA_BLOCK = 256
MOBA_TOPK = 3
MAX_BLOCKS = 32
SSM_HEADS = 32
SSM_PAIRS = SSM_HEADS // 2
SSM_GROUPS = 4
SSM_STATE = 128
SSM_CONV = 4
SSM_CHUNK = 128
CONV_PAD = 8
NEG = -1e30
SAFE_EXP2_RANGE = 60.0
VMEM_LIMIT = 56 * 1024 * 1024


def _sigmoid(v):
    return 1.0 / (1.0 + jnp.exp(-v))


def _nt_dot(a, b, **kw):
    return lax.dot_general(a, b, (((1,), (1,)), ((), ())), preferred_element_type=F32, **kw)


def _inproj_kernel(x_ref, g_ref, w_ref, wdt_ref, o_ref, dt_ref, h_ref):
    @pl.when(pl.program_id(1) == 0)
    def _():
        x = x_ref[...]
        ms = jnp.mean(x * x, axis=-1, keepdims=True)
        h = (x * lax.rsqrt(ms + EPS) * g_ref[...]).astype(BF16)
        h_ref[...] = h
        dt_ref[...] = jnp.dot(h, wdt_ref[...], preferred_element_type=F32)

    o_ref[...] = _nt_dot(h_ref[...], w_ref[...]).astype(o_ref.dtype)


def _in_proj(x2d, g, w_t, row_starts, w_dt, tm=2048, tn=1024):
    m, d = x2d.shape
    n = len(row_starts) * tn

    def weight_rows(i, j):
        start = row_starts[-1]
        for k, r in enumerate(row_starts[:-1]):
            start = jnp.where(j == k, r, start)
        return pl.multiple_of(start, math.gcd(*row_starts[1:])), 0

    return pl.pallas_call(
        _inproj_kernel,
        grid=(m // tm, n // tn),
        in_specs=[
            pl.BlockSpec((tm, d), lambda i, j: (i, 0)),
            pl.BlockSpec((1, d), lambda i, j: (0, 0)),
            pl.BlockSpec((pl.Element(tn), pl.Element(d)), weight_rows),
            pl.BlockSpec((d, LANES), lambda i, j: (0, 0)),
        ],
        out_specs=[
            pl.BlockSpec((tm, tn), lambda i, j: (i, j)),
            pl.BlockSpec((tm, LANES), lambda i, j: (i, 0)),
        ],
        out_shape=[jax.ShapeDtypeStruct((m, n), BF16), jax.ShapeDtypeStruct((m, LANES), F32)],
        scratch_shapes=[pltpu.VMEM((tm, d), BF16)],
        compiler_params=pltpu.CompilerParams(
            dimension_semantics=("parallel", "arbitrary"), vmem_limit_bytes=VMEM_LIMIT),
        name="in_proj",
    )(x2d, g, w_t, w_dt)


def _head_rms_norm(x, gain, is_a):
    outs = []
    for c in range(x.shape[1] // LANES):
        blk = x[:, c * LANES:(c + 1) * LANES]
        sq = blk * blk
        sa = jnp.sum(jnp.where(is_a, sq, 0.0), axis=-1, keepdims=True)
        sb = jnp.sum(jnp.where(is_a, 0.0, sq), axis=-1, keepdims=True)
        ms = jnp.where(is_a, sa, sb) * (1.0 / HEAD_DIM)
        outs.append(blk * lax.rsqrt(ms + EPS))
    return jnp.concatenate(outs, axis=1) * gain


def _prep_kernel(q_ref, k_ref, v_ref, qg_ref, kg_ref, qn_ref, kn_ref, sel_ref, vt_ref, kmean_ref):
    i = pl.program_id(1)

    @pl.when(i == 0)
    def _():
        kmean_ref[...] = jnp.zeros_like(kmean_ref)

    lane = lax.broadcasted_iota(jnp.int32, (1, LANES), 1)
    is_a = lane < HEAD_DIM
    qn = _head_rms_norm(q_ref[...].astype(F32), qg_ref[...], is_a) * (HEAD_DIM ** -0.5 * LOG2E)
    kn = _head_rms_norm(k_ref[...].astype(F32), kg_ref[...], is_a)
    kn_ref[...] = kn.astype(BF16)

    vt = v_ref[...].astype(F32).T
    vt_ref[...] = vt.reshape(vt_ref.shape).astype(BF16)

    kmean = kmean_ref[...]
    q_hi = qn.astype(BF16)
    qn_ref[...] = q_hi
    q_lo = (qn - q_hi.astype(F32)).astype(BF16)
    km_hi = kmean.astype(BF16)
    km_lo = (kmean - km_hi.astype(F32)).astype(BF16)
    zero = jnp.zeros((), BF16)
    jidx = lax.broadcasted_iota(jnp.int32, (MAX_BLOCKS, MOBA_BLOCK), 0).astype(F32)
    valid = jidx < i.astype(F32)
    for p in range(HEAD_PAIRS):
        cols = slice(p * LANES, (p + 1) * LANES)
        km4 = jnp.concatenate([jnp.where(is_a, km_hi[:, cols], zero), jnp.where(is_a, zero, km_hi[:, cols]),
                               jnp.where(is_a, km_lo[:, cols], zero), jnp.where(is_a, zero, km_lo[:, cols])],
                              axis=0)
        g_hi = _nt_dot(km4, q_hi[:, cols])
        g_lo = _nt_dot(km4[:2 * MAX_BLOCKS], q_lo[:, cols])
        for hh in range(2):
            r0 = hh * MAX_BLOCKS
            gate = (g_hi[r0:r0 + MAX_BLOCKS] + g_hi[r0 + 2 * MAX_BLOCKS:r0 + 3 * MAX_BLOCKS]
                    + g_lo[r0:r0 + MAX_BLOCKS])
            gate = jnp.where(valid, gate, -jnp.inf)
            sel = jnp.zeros_like(gate)
            for _ in range(MOBA_TOPK):
                top = jnp.max(gate, axis=0, keepdims=True)
                first = jnp.min(jnp.where(gate == top, jidx, float(MAX_BLOCKS)), axis=0, keepdims=True)
                pick = jidx == first
                sel = jnp.where(pick, 1.0, sel)
                gate = jnp.where(pick, -jnp.inf, gate)
            sel_ref[2 * p + hh] = jnp.where(valid, sel, 0.0)

    kmean_ref[pl.ds(i, 1), :] = jnp.mean(kn, axis=0, keepdims=True)


def _attn_prep(proj3, qg, kg):
    b, s, _ = proj3.shape
    nb = s // MOBA_BLOCK
    width = ATTN_HEADS * HEAD_DIM
    tile = lambda col: pl.BlockSpec((None, MOBA_BLOCK, width), lambda bb, i, col=col: (bb, i, col))
    gain = pl.BlockSpec((1, width), lambda bb, i: (0, 0))
    return pl.pallas_call(
        _prep_kernel,
        grid=(b, nb),
        in_specs=[tile(0), tile(1), tile(2), gain, gain],
        out_specs=[
            tile(0), tile(0),
            pl.BlockSpec((None, ATTN_HEADS, None, MAX_BLOCKS, MOBA_BLOCK), lambda bb, i: (bb, 0, i, 0, 0)),
            pl.BlockSpec((None, HEAD_PAIRS, None, LANES, MOBA_BLOCK), lambda bb, i: (bb, 0, i, 0, 0)),
        ],
        out_shape=[
            jax.ShapeDtypeStruct((b, s, width), BF16),
            jax.ShapeDtypeStruct((b, s, width), BF16),
            jax.ShapeDtypeStruct((b, ATTN_HEADS, nb, MAX_BLOCKS, MOBA_BLOCK), F32),
            jax.ShapeDtypeStruct((b, HEAD_PAIRS, nb, LANES, MOBA_BLOCK), BF16),
        ],
        scratch_shapes=[pltpu.VMEM((MAX_BLOCKS, width), F32)],
        compiler_params=pltpu.CompilerParams(
            dimension_semantics=("parallel", "arbitrary"), vmem_limit_bytes=VMEM_LIMIT),
        name="attn_prep",
    )(proj3, proj3, proj3, qg, kg)


def _attn_kernel(bounded_ref, q_ref, k_ref, vt_ref, sel_ref, o_ref, qm_ref, acc_ref, s_ref, p_ref):
    i = pl.program_id(2)
    heads = 2 * ATTN_PAIRS_PER_STEP
    lane = lax.broadcasted_iota(jnp.int32, (1, LANES), 1)
    is_a = lane < HEAD_DIM
    zero = jnp.zeros((), BF16)
    for pp in range(ATTN_PAIRS_PER_STEP):
        qp = q_ref[:, pp * LANES:(pp + 1) * LANES]
        qm_ref[2 * pp] = jnp.where(is_a, qp, zero)
        qm_ref[2 * pp + 1] = jnp.where(is_a, zero, qp)

    def scores(j, h):
        pp = h // 2
        kt = k_ref[pl.ds(pl.multiple_of(j * MOBA_BLOCK, MOBA_BLOCK), MOBA_BLOCK), pp * LANES:(pp + 1) * LANES]
        return _nt_dot(kt, qm_ref[h])

    def values(j, h):
        return vt_ref[h // 2, j, (h % 2) * HEAD_DIM:(h % 2 + 1) * HEAD_DIM, :]

    krow = lax.broadcasted_iota(jnp.int32, (MOBA_BLOCK, MOBA_BLOCK), 0)
    qcol = lax.broadcasted_iota(jnp.int32, (MOBA_BLOCK, MOBA_BLOCK), 1)
    causal = krow <= qcol
    zero_row = jnp.zeros((1, MOBA_BLOCK), F32)

    def bounded_scores():
        ls = [zero_row] * heads
        for h in range(heads):
            p = jnp.exp2(jnp.where(causal, scores(i, h), NEG))
            ls[h] = jnp.sum(p, axis=0, keepdims=True)
            p_ref[0, h] = p.astype(BF16)
        for h in range(heads):
            acc_ref[h] = jnp.dot(values(i, h), p_ref[0, h], preferred_element_type=F32)

        def body(jj, ls):
            ls = list(ls)
            for u in range(2):
                j = 2 * jj + u
                for h in range(heads):
                    picked = sel_ref[h, pl.ds(j, 1), :] > 0.0
                    if u == 1:
                        picked = jnp.logical_and(picked, j < i)
                    p = jnp.exp2(scores(j, h) + jnp.where(picked, 0.0, NEG))
                    ls[h] = ls[h] + jnp.sum(p, axis=0, keepdims=True)
                    p_ref[u, h] = p.astype(BF16)
            for u in range(2):
                for h in range(heads):
                    acc_ref[h] += jnp.dot(values(2 * jj + u, h), p_ref[u, h], preferred_element_type=F32)
            return tuple(ls)

        return lax.fori_loop(0, (i + 1) // 2, body, tuple(ls))

    def running_max():
        m0, l0 = [], []
        for h in range(heads):
            s = jnp.where(causal, scores(i, h), NEG)
            m0.append(jnp.max(s, axis=0, keepdims=True))
            s_ref[h] = s
        for h in range(heads):
            p = jnp.exp2(s_ref[h] - m0[h])
            l0.append(jnp.sum(p, axis=0, keepdims=True))
            p_ref[0, h] = p.astype(BF16)
        for h in range(heads):
            acc_ref[h] = jnp.dot(values(i, h), p_ref[0, h], preferred_element_type=F32)

        def body(j, carry):
            ms, ls = carry
            tile_max, new_m, new_l, alphas = [], [], [], []
            for h in range(heads):
                s = scores(j, h)
                tile_max.append(jnp.max(s, axis=0, keepdims=True))
                s_ref[h] = s
            for h in range(heads):
                picked = sel_ref[h, pl.ds(j, 1), :] > 0.0
                m = jnp.maximum(ms[h], jnp.where(picked, tile_max[h], NEG))
                p = jnp.exp2(s_ref[h] - jnp.where(picked, m, -NEG))
                alpha = jnp.exp2(ms[h] - m)
                new_m.append(m)
                new_l.append(alpha * ls[h] + jnp.sum(p, axis=0, keepdims=True))
                alphas.append(alpha)
                p_ref[0, h] = p.astype(BF16)
            for h in range(heads):
                acc_ref[h] = alphas[h] * acc_ref[h] + jnp.dot(values(j, h), p_ref[0, h],
                                                              preferred_element_type=F32)
            return tuple(new_m), tuple(new_l)

        return lax.fori_loop(0, i, body, (tuple(m0), tuple(l0)))[1]

    ls = lax.cond(bounded_ref[0] != 0, bounded_scores, running_max)
    for pp in range(ATTN_PAIRS_PER_STEP):
        out_t = jnp.concatenate([acc_ref[2 * pp] * (1.0 / ls[2 * pp]),
                                 acc_ref[2 * pp + 1] * (1.0 / ls[2 * pp + 1])], axis=0)
        o_ref[:, pp * LANES:(pp + 1) * LANES] = out_t.T.astype(o_ref.dtype)


def _moba_attention(bounded, qn, kn, sel, vt5):
    b, s, width = qn.shape
    nb = s // MOBA_BLOCK
    pps = ATTN_PAIRS_PER_STEP
    gw = pps * LANES
    qtile = pl.BlockSpec((None, MOBA_BLOCK, gw), lambda bb, g, i: (bb, i, g))
    return pl.pallas_call(
        _attn_kernel,
        grid=(b, HEAD_PAIRS // pps, nb),
        in_specs=[
            pl.BlockSpec(memory_space=pltpu.SMEM),
            qtile,
            pl.BlockSpec((None, s, gw), lambda bb, g, i: (bb, 0, g), pipeline_mode=pl.Buffered(1)),
            pl.BlockSpec((None, pps, nb, LANES, MOBA_BLOCK), lambda bb, g, i: (bb, g, 0, 0, 0),
                         pipeline_mode=pl.Buffered(1)),
            pl.BlockSpec((None, 2 * pps, None, MAX_BLOCKS, MOBA_BLOCK), lambda bb, g, i: (bb, g, i, 0, 0)),
        ],
        out_specs=qtile,
        out_shape=jax.ShapeDtypeStruct((b, s, width), BF16),
        scratch_shapes=[
            pltpu.VMEM((2 * pps, MOBA_BLOCK, LANES), BF16),
            pltpu.VMEM((2 * pps, HEAD_DIM, MOBA_BLOCK), F32),
            pltpu.VMEM((2 * pps, MOBA_BLOCK, MOBA_BLOCK), F32),
            pltpu.VMEM((2, 2 * pps, MOBA_BLOCK, MOBA_BLOCK), BF16),
        ],
        compiler_params=pltpu.CompilerParams(
            dimension_semantics=("parallel", "parallel", "arbitrary"), vmem_limit_bytes=VMEM_LIMIT),
        name="moba_attn",
    )(bounded, qn, kn, vt5, sel)


def _ssd_kernel(xbc_ref, z_ref, dt_ref, cw_ref, cb_ref, dtb_ref, alog_ref, dsk_ref, ng_ref, ec_ref, ew_ref,
                o_ref, ext_ref, act_ref, state_ref, y_ref, acst_ref, ldtt_ref, rhs_ref, ly_ref, ls_ref):
    c = pl.program_id(1)
    L = SSM_CHUNK
    n_blk = act_ref.shape[0]
    b_blk = SSM_PAIRS
    c_blk = SSM_PAIRS + SSM_GROUPS

    @pl.when(c == 0)
    def _():
        ext_ref[:, 0:CONV_PAD, :] = jnp.zeros((n_blk, CONV_PAD, LANES), F32)
        state_ref[...] = jnp.zeros_like(state_ref)

    def conv_silu(blk):
        ext_ref[blk, CONV_PAD:CONV_PAD + L, :] = xbc_ref[:, blk * LANES:(blk + 1) * LANES].astype(F32)
        w = cw_ref[blk]
        conv = cb_ref[blk]
        for k in range(SSM_CONV):
            start = CONV_PAD - (SSM_CONV - 1) + k
            conv = conv + w[k:k + 1, :] * ext_ref[blk, start:start + L, :]
        ext_ref[blk, 0:CONV_PAD, :] = ext_ref[blk, L:L + CONV_PAD, :]
        act_ref[blk] = conv * _sigmoid(conv)

    dt_in = dt_ref[...] + dtb_ref[...]
    dtv = jnp.maximum(dt_in, 0.0) + jnp.log(1.0 + jnp.exp(-jnp.abs(dt_in)))
    a2 = dtv * (-LOG2E * jnp.exp(alog_ref[...]))
    row = lax.broadcasted_iota(jnp.int32, (L, L), 0)
    col = lax.broadcasted_iota(jnp.int32, (L, L), 1)
    tri = col <= row
    acs = jnp.dot(tri.astype(F32), a2, preferred_element_type=F32, precision=lax.Precision.HIGHEST)
    acst_ref[...] = acs.T
    ldtt_ref[...] = (jnp.log(dtv) * LOG2E).T
    lane = lax.broadcasted_iota(jnp.int32, (1, LANES), 1)
    is_a = lane < HEAD_DIM
    zero = jnp.zeros((), BF16)

    def split3(v):
        hi = v.astype(BF16)
        r1 = v - hi.astype(F32)
        mid = r1.astype(BF16)
        lo = (r1 - mid.astype(F32)).astype(BF16)
        return jnp.where(lane < SSM_HEADS, hi, jnp.where(lane < 2 * SSM_HEADS, mid, lo))

    acs_parts = split3(acs)
    wst_parts = split3(jnp.exp2(acs))
    ppg = SSM_PAIRS // SSM_GROUPS

    for g in range(SSM_GROUPS):
        for blk in list(range(g * ppg, (g + 1) * ppg)) + [b_blk + g, c_blk + g]:
            conv_silu(blk)
        bg = act_ref[b_blk + g]
        cbm = _nt_dot(act_ref[c_blk + g].astype(BF16), bg.astype(BF16))
        bt = bg.T
        cmat = jnp.dot(acs_parts, ec_ref[g], preferred_element_type=F32)
        wst = jnp.dot(wst_parts, ew_ref[g], preferred_element_type=F32)
        prev = jnp.concatenate([state_ref[g * ppg + pj] for pj in range(ppg)], axis=1).astype(BF16)
        y_off = jnp.dot(act_ref[c_blk + g].astype(BF16), prev, preferred_element_type=F32) * wst
        decs = []
        for pj in range(ppg):
            xp = act_ref[g * ppg + pj].astype(BF16)
            rhs_ref[g * ppg + pj] = jnp.concatenate([jnp.where(is_a, xp, zero), jnp.where(is_a, zero, xp)], axis=0)
            lhs_y, lhs_s, dec = [], [], []
            for hh in range(2):
                k = 2 * pj + hh
                h = g * 2 * ppg + k
                arow = acst_ref[h:h + 1, :]
                rrow = arow - ldtt_ref[h:h + 1, :]
                expo = jnp.where(tri, cmat[:, k * LANES:(k + 1) * LANES] - rrow, NEG)
                lhs_y.append((jnp.exp2(expo) * cbm).astype(BF16))
                tot = arow[:, L - 1:L]
                lhs_s.append((bt * jnp.exp2(tot - rrow)).astype(BF16))
                dec.append(jnp.exp2(tot))
            ly_ref[g * ppg + pj] = jnp.concatenate(lhs_y, axis=1)
            ls_ref[g * ppg + pj] = jnp.concatenate(lhs_s, axis=1)
            decs.append(jnp.where(is_a, dec[0], dec[1]))
        for pj in range(ppg):
            y_ref[g * ppg + pj] = (jnp.dot(ly_ref[g * ppg + pj], rhs_ref[g * ppg + pj], preferred_element_type=F32)
                                   + y_off[:, pj * LANES:(pj + 1) * LANES])
        for pj in range(ppg):
            j = g * ppg + pj
            state_ref[j] = state_ref[j] * decs[pj] + jnp.dot(ls_ref[g * ppg + pj], rhs_ref[g * ppg + pj],
                                                            preferred_element_type=F32)

        ys = []
        ssq = jnp.zeros((L, 1), F32)
        for j in range(g * ppg, (g + 1) * ppg):
            cols = slice(j * LANES, (j + 1) * LANES)
            zf = z_ref[:, cols].astype(F32)
            y = (y_ref[j] + dsk_ref[:, cols] * act_ref[j]) * (zf * _sigmoid(zf))
            ssq = ssq + jnp.sum(y * y, axis=-1, keepdims=True)
            ys.append(y)
        scale = lax.rsqrt(ssq * (1.0 / (ppg * LANES)) + EPS)
        for j, y in zip(range(g * ppg, (g + 1) * ppg), ys):
            cols = slice(j * LANES, (j + 1) * LANES)
            o_ref[:, cols] = (y * scale * ng_ref[:, cols]).astype(o_ref.dtype)


def _head_expansion(width):
    r = lax.broadcasted_iota(jnp.int32, (LANES, SSM_HEADS * width), 0)
    col = lax.broadcasted_iota(jnp.int32, (LANES, SSM_HEADS * width), 1)
    e = jnp.logical_and(r < 3 * SSM_HEADS, r % SSM_HEADS == col // width).astype(BF16)
    return e.reshape(LANES, SSM_GROUPS, -1).transpose(1, 0, 2)


def _ssd(proj3, dt3, conv_w, conv_b, dt_bias, a_log, d_skip_x, norm_g):
    b, s, _ = proj3.shape
    L = SSM_CHUNK
    inner = SSM_HEADS * HEAD_DIM
    conv_dim = inner + 2 * SSM_GROUPS * SSM_STATE
    n_blk = conv_dim // LANES
    hpg = SSM_HEADS // SSM_GROUPS
    ppg = SSM_PAIRS // SSM_GROUPS
    vec = lambda w: pl.BlockSpec((1, w), lambda bb, c: (0, 0))
    return pl.pallas_call(
        _ssd_kernel,
        grid=(b, s // L),
        in_specs=[
            pl.BlockSpec((None, L, conv_dim), lambda bb, c: (bb, c, 1)),
            pl.BlockSpec((None, L, inner), lambda bb, c: (bb, c, 3)),
            pl.BlockSpec((None, L, LANES), lambda bb, c: (bb, c, 0)),
            pl.BlockSpec((n_blk, SSM_CONV, LANES), lambda bb, c: (0, 0, 0)),
            pl.BlockSpec((n_blk, 1, LANES), lambda bb, c: (0, 0, 0)),
            vec(LANES), vec(LANES), vec(inner), vec(inner),
            pl.BlockSpec((SSM_GROUPS, LANES, hpg * L), lambda bb, c: (0, 0, 0)),
            pl.BlockSpec((SSM_GROUPS, LANES, hpg * HEAD_DIM), lambda bb, c: (0, 0, 0)),
        ],
        out_specs=pl.BlockSpec((None, L, inner), lambda bb, c: (bb, c, 0)),
        out_shape=jax.ShapeDtypeStruct((b, s, inner), BF16),
        scratch_shapes=[
            pltpu.VMEM((n_blk, L + CONV_PAD, LANES), F32),
            pltpu.VMEM((n_blk, L, LANES), F32),
            pltpu.VMEM((SSM_PAIRS, SSM_STATE, LANES), F32),
            pltpu.VMEM((SSM_PAIRS, L, LANES), F32),
            pltpu.VMEM((LANES, L), F32),
            pltpu.VMEM((LANES, L), F32),
            pltpu.VMEM((SSM_PAIRS, 2 * L, LANES), BF16),
            pltpu.VMEM((SSM_PAIRS, L, 2 * L), BF16),
            pltpu.VMEM((SSM_PAIRS, SSM_STATE, 2 * L), BF16),
        ],
        compiler_params=pltpu.CompilerParams(
            dimension_semantics=("parallel", "arbitrary"), vmem_limit_bytes=VMEM_LIMIT),
        name="ssd",
    )(proj3, proj3, dt3, conv_w, conv_b, dt_bias, a_log, d_skip_x, norm_g,
      _head_expansion(L), _head_expansion(HEAD_DIM))


def _mix_kernel(att_ref, yn_ref, ga_ref, gb_ref, x_ref, woa_ref, wos_ref, wout_ref, o_ref):
    ya = jnp.dot(att_ref[...], woa_ref[...], preferred_element_type=F32)
    yb = jnp.dot(yn_ref[...], wos_ref[...], preferred_element_type=F32)
    merged = _sigmoid(ga_ref[...].astype(F32)) * ya + _sigmoid(gb_ref[...].astype(F32)) * yb
    o_ref[...] = x_ref[...] + jnp.dot(merged.astype(BF16), wout_ref[...], preferred_element_type=F32)


def _mix_out(att2d, yn2d, proj2d, x2d, w_oa, w_os, w_out, tm=512):
    m, d = x2d.shape
    ga_col = (proj2d.shape[1] - 2 * d) // d
    row = lambda w, col=0: pl.BlockSpec((tm, w), lambda i, col=col: (i, col))
    full = lambda a: pl.BlockSpec(a.shape, lambda i: (0, 0))
    return pl.pallas_call(
        _mix_kernel,
        grid=(m // tm,),
        in_specs=[row(att2d.shape[1]), row(yn2d.shape[1]), row(d, ga_col), row(d, ga_col + 1), row(d),
                  full(w_oa), full(w_os), full(w_out)],
        out_specs=row(d),
        out_shape=jax.ShapeDtypeStruct((m, d), F32),
        compiler_params=pltpu.CompilerParams(
            dimension_semantics=("parallel",), vmem_limit_bytes=VMEM_LIMIT),
        name="mix_out",
    )(att2d, yn2d, proj2d, proj2d, x2d, w_oa, w_os, w_out)


def _rms(x, g):
    return x * lax.rsqrt(jnp.mean(x * x, axis=-1, keepdims=True) + EPS) * g


def _ffn_kernel(x_ref, g2_ref, wg_ref, wu_ref, wd_ref, p_ref, g3_ref, wpg_ref, wpp_ref, o_ref,
                h_ref, acc_ref):
    j = pl.program_id(1)

    @pl.when(j == 0)
    def _():
        h_ref[...] = _rms(x_ref[...], g2_ref[...]).astype(BF16)
        acc_ref[...] = jnp.zeros_like(acc_ref)

    h = h_ref[...]
    gate = jnp.dot(h, wg_ref[...], preferred_element_type=F32)
    up = jnp.dot(h, wu_ref[...], preferred_element_type=F32)
    act = (gate * _sigmoid(gate) * up).astype(BF16)
    acc_ref[...] += jnp.dot(act, wd_ref[...], preferred_element_type=F32)

    @pl.when(j == pl.num_programs(1) - 1)
    def _():
        x2 = x_ref[...] + acc_ref[...]
        h3 = _rms(x2, g3_ref[...]).astype(BF16)
        pgate = _sigmoid(jnp.dot(h3, wpg_ref[...], preferred_element_type=F32))
        proj = jnp.dot(p_ref[...].astype(BF16), wpp_ref[...], preferred_element_type=F32)
        o_ref[...] = x2 + proj * pgate


def _ffn_ple(x2d, g2, w_gu, w_down, p2d, g3, w_pg, w_pp, tm=512, n_ff=2):
    m, d = x2d.shape
    d_ff = w_down.shape[0]
    tf = d_ff // n_ff
    const = lambda a: pl.BlockSpec(a.shape, lambda i, j: (0, 0))
    return pl.pallas_call(
        _ffn_kernel,
        grid=(m // tm, n_ff),
        in_specs=[
            pl.BlockSpec((tm, d), lambda i, j: (i, 0)),
            const(g2),
            pl.BlockSpec((d, tf), lambda i, j: (0, j)),
            pl.BlockSpec((d, tf), lambda i, j: (0, n_ff + j)),
            pl.BlockSpec((tf, d), lambda i, j: (j, 0)),
            pl.BlockSpec((tm, p2d.shape[1]), lambda i, j: (i, 0)),
            const(g3), const(w_pg), const(w_pp),
        ],
        out_specs=pl.BlockSpec((tm, d), lambda i, j: (i, 0)),
        out_shape=jax.ShapeDtypeStruct((m, d), F32),
        scratch_shapes=[pltpu.VMEM((tm, d), BF16), pltpu.VMEM((tm, d), F32)],
        compiler_params=pltpu.CompilerParams(
            dimension_semantics=("parallel", "arbitrary"), vmem_limit_bytes=VMEM_LIMIT),
        name="ffn_ple",
    )(x2d, g2, w_gu, w_gu, w_down, p2d, g3, w_pg, w_pp)


def _head_lanes(v):
    return jnp.pad(jnp.tile(v.astype(F32), 3), (0, LANES - 3 * SSM_HEADS)).reshape(1, LANES)


def _lane_blocks(a):
    r, w = a.shape
    return a.astype(F32).reshape(r, w // LANES, LANES).transpose(1, 0, 2)


def kernel(x, p, ln1_g, w_in, q_norm_g, k_norm_g, w_o_attn, conv_w, conv_b, dt_bias, a_log, d_skip,
           ssm_norm_g, w_o_ssm, w_out, ln2_g, w_gate_up, w_down, ln3_g, w_ple_gate, w_ple_proj):
    b, s, d = x.shape
    assert s % MOBA_BLOCK == 0 and s // MOBA_BLOCK <= MAX_BLOCKS
    attn_w = ATTN_HEADS * HEAD_DIM
    inner = SSM_HEADS * HEAD_DIM
    conv_dim = inner + 2 * SSM_GROUPS * SSM_STATE
    off_z = 3 * attn_w
    off_xbc = off_z + inner
    off_dt = off_xbc + conv_dim
    off_g = off_dt + SSM_HEADS
    row = lambda v: v.astype(F32).reshape(1, -1)

    for i in range(w_in.shape[0]):
        wi = w_in[i]
        tn = 1024
        row_starts = [r for lo, hi in ((0, off_z), (off_xbc, off_dt), (off_z, off_xbc), (off_g, wi.shape[1]))
                      for r in range(lo, hi, tn)]
        w_t = wi.T.astype(BF16)
        w_dt = jnp.pad(jnp.tile(wi[:, off_dt:off_g], (1, 3)), ((0, 0), (0, LANES - 3 * SSM_HEADS))).astype(BF16)

        x2d = x.reshape(b * s, d)
        proj, dt_raw = _in_proj(x2d, row(ln1_g[i]), w_t, row_starts, w_dt, tn=tn)
        proj3 = proj.reshape(b, s, -1)

        qn, kn, sel, vt5 = _attn_prep(
            proj3, row(jnp.tile(q_norm_g[i], ATTN_HEADS)), row(jnp.tile(k_norm_g[i], ATTN_HEADS)))
        score_bound = 1.02 * LOG2E * HEAD_DIM ** 0.5 * jnp.max(jnp.abs(q_norm_g[i])) * jnp.max(jnp.abs(k_norm_g[i]))
        bounded = (score_bound <= SAFE_EXP2_RANGE).astype(jnp.int32).reshape(1)
        att = _moba_attention(bounded, qn, kn, sel, vt5)

        yn = _ssd(proj3, dt_raw.reshape(b, s, LANES), _lane_blocks(conv_w[i]), _lane_blocks(row(conv_b[i])),
                  _head_lanes(dt_bias[i]), _head_lanes(a_log[i]), row(jnp.repeat(d_skip[i], HEAD_DIM)),
                  row(ssm_norm_g[i]))

        x1 = _mix_out(att.reshape(b * s, attn_w), yn.reshape(b * s, inner), proj, x2d,
                      w_o_attn[i].astype(BF16), w_o_ssm[i].astype(BF16), w_out[i].astype(BF16))
        x2 = _ffn_ple(x1, row(ln2_g[i]), w_gate_up[i].astype(BF16), w_down[i].astype(BF16),
                      p[i].reshape(b * s, -1), row(ln3_g[i]), w_ple_gate[i].astype(BF16),
                      w_ple_proj[i].astype(BF16))
        x = x2.reshape(b, s, d)
    return x
```

```python
import math

import jax
import jax.numpy as jnp
from jax import lax
from jax.experimental import pallas as pl
from jax.experimental.pallas import tpu as pltpu

F32 = jnp.float32
BF16 = jnp.bfloat16

EPS = 1e-6
LOG2E = math.log2(math.e)
LANES = 128
HEAD_DIM = 64
ATTN_HEADS = 16
HEAD_PAIRS = ATTN_HEADS // 2
ATTN_PAIRS_PER_STEP = 8
MOBA_BLOCK = 256
MOBA_TOPK = 3
MAX_BLOCKS = 32
SSM_HEADS = 32
SSM_PAIRS = SSM_HEADS // 2
SSM_GROUPS = 4
SSM_STATE = 128
SSM_CONV = 4
SSM_CHUNK = 128
CONV_PAD = 8
NEG = -1e30
SAFE_EXP2_RANGE = 60.0
VMEM_LIMIT = 56 * 1024 * 1024


def _sigmoid(v):
    return 1.0 / (1.0 + jnp.exp(-v))


def _nt_dot(a, b, **kw):
    return lax.dot_general(a, b, (((1,), (1,)), ((), ())), preferred_element_type=F32, **kw)


def _inproj_kernel(x_ref, g_ref, w_ref, wdt_ref, o_ref, dt_ref, h_ref):
    @pl.when(pl.program_id(1) == 0)
    def _():
        x = x_ref[...]
        ms = jnp.mean(x * x, axis=-1, keepdims=True)
        h = (x * lax.rsqrt(ms + EPS) * g_ref[...]).astype(BF16)
        h_ref[...] = h
        dt_ref[...] = jnp.dot(h, wdt_ref[...], preferred_element_type=F32)

    o_ref[...] = _nt_dot(h_ref[...], w_ref[...]).astype(o_ref.dtype)


def _in_proj(x2d, g, w_t, row_starts, w_dt, tm=2048, tn=1024):
    m, d = x2d.shape
    n = len(row_starts) * tn

    def weight_rows(i, j):
        start = row_starts[-1]
        for k, r in enumerate(row_starts[:-1]):
            start = jnp.where(j == k, r, start)
        return pl.multiple_of(start, math.gcd(*row_starts[1:])), 0

    return pl.pallas_call(
        _inproj_kernel,
        grid=(m // tm, n // tn),
        in_specs=[
            pl.BlockSpec((tm, d), lambda i, j: (i, 0)),
            pl.BlockSpec((1, d), lambda i, j: (0, 0)),
            pl.BlockSpec((pl.Element(tn), pl.Element(d)), weight_rows),
            pl.BlockSpec((d, LANES), lambda i, j: (0, 0)),
        ],
        out_specs=[
            pl.BlockSpec((tm, tn), lambda i, j: (i, j)),
            pl.BlockSpec((tm, LANES), lambda i, j: (i, 0)),
        ],
        out_shape=[jax.ShapeDtypeStruct((m, n), BF16), jax.ShapeDtypeStruct((m, LANES), F32)],
        scratch_shapes=[pltpu.VMEM((tm, d), BF16)],
        compiler_params=pltpu.CompilerParams(
            dimension_semantics=("parallel", "arbitrary"), vmem_limit_bytes=VMEM_LIMIT),
        name="in_proj",
    )(x2d, g, w_t, w_dt)


def _head_rms_norm(x, gain, is_a):
    outs = []
    for c in range(x.shape[1] // LANES):
        blk = x[:, c * LANES:(c + 1) * LANES]
        sq = blk * blk
        sa = jnp.sum(jnp.where(is_a, sq, 0.0), axis=-1, keepdims=True)
        sb = jnp.sum(jnp.where(is_a, 0.0, sq), axis=-1, keepdims=True)
        ms = jnp.where(is_a, sa, sb) * (1.0 / HEAD_DIM)
        outs.append(blk * lax.rsqrt(ms + EPS))
    return jnp.concatenate(outs, axis=1) * gain


def _prep_kernel(q_ref, k_ref, v_ref, qg_ref, kg_ref, qn_ref, kn_ref, sel_ref, vt_ref, kmean_ref):
    i = pl.program_id(1)

    @pl.when(i == 0)
    def _():
        kmean_ref[...] = jnp.zeros_like(kmean_ref)

    lane = lax.broadcasted_iota(jnp.int32, (1, LANES), 1)
    is_a = lane < HEAD_DIM
    qn = _head_rms_norm(q_ref[...].astype(F32), qg_ref[...], is_a) * (HEAD_DIM ** -0.5 * LOG2E)
    kn = _head_rms_norm(k_ref[...].astype(F32), kg_ref[...], is_a)
    kn_ref[...] = kn.astype(BF16)

    vt = v_ref[...].astype(F32).T
    vt_ref[...] = vt.reshape(vt_ref.shape).astype(BF16)

    kmean = kmean_ref[...]
    q_hi = qn.astype(BF16)
    qn_ref[...] = q_hi
    q_lo = (qn - q_hi.astype(F32)).astype(BF16)
    km_hi = kmean.astype(BF16)
    km_lo = (kmean - km_hi.astype(F32)).astype(BF16)
    zero = jnp.zeros((), BF16)
    jidx = lax.broadcasted_iota(jnp.int32, (MAX_BLOCKS, MOBA_BLOCK), 0).astype(F32)
    valid = jidx < i.astype(F32)
    for p in range(HEAD_PAIRS):
        cols = slice(p * LANES, (p + 1) * LANES)
        km4 = jnp.concatenate([jnp.where(is_a, km_hi[:, cols], zero), jnp.where(is_a, zero, km_hi[:, cols]),
                               jnp.where(is_a, km_lo[:, cols], zero), jnp.where(is_a, zero, km_lo[:, cols])],
                              axis=0)
        g_hi = _nt_dot(km4, q_hi[:, cols])
        g_lo = _nt_dot(km4[:2 * MAX_BLOCKS], q_lo[:, cols])
        for hh in range(2):
            r0 = hh * MAX_BLOCKS
            gate = (g_hi[r0:r0 + MAX_BLOCKS] + g_hi[r0 + 2 * MAX_BLOCKS:r0 + 3 * MAX_BLOCKS]
                    + g_lo[r0:r0 + MAX_BLOCKS])
            gate = jnp.where(valid, gate, -jnp.inf)
            sel = jnp.zeros_like(gate)
            for _ in range(MOBA_TOPK):
                top = jnp.max(gate, axis=0, keepdims=True)
                first = jnp.min(jnp.where(gate == top, jidx, float(MAX_BLOCKS)), axis=0, keepdims=True)
                pick = jidx == first
                sel = jnp.where(pick, 1.0, sel)
                gate = jnp.where(pick, -jnp.inf, gate)
            sel_ref[2 * p + hh] = jnp.where(valid, sel, 0.0)

    kmean_ref[pl.ds(i, 1), :] = jnp.mean(kn, axis=0, keepdims=True)


def _attn_prep(proj3, qg, kg):
    b, s, _ = proj3.shape
    nb = s // MOBA_BLOCK
    width = ATTN_HEADS * HEAD_DIM
    tile = lambda col: pl.BlockSpec((None, MOBA_BLOCK, width), lambda bb, i, col=col: (bb, i, col))
    gain = pl.BlockSpec((1, width), lambda bb, i: (0, 0))
    return pl.pallas_call(
        _prep_kernel,
        grid=(b, nb),
        in_specs=[tile(0), tile(1), tile(2), gain, gain],
        out_specs=[
            tile(0), tile(0),
            pl.BlockSpec((None, ATTN_HEADS, None, MAX_BLOCKS, MOBA_BLOCK), lambda bb, i: (bb, 0, i, 0, 0)),
            pl.BlockSpec((None, HEAD_PAIRS, None, LANES, MOBA_BLOCK), lambda bb, i: (bb, 0, i, 0, 0)),
        ],
        out_shape=[
            jax.ShapeDtypeStruct((b, s, width), BF16),
            jax.ShapeDtypeStruct((b, s, width), BF16),
            jax.ShapeDtypeStruct((b, ATTN_HEADS, nb, MAX_BLOCKS, MOBA_BLOCK), F32),
            jax.ShapeDtypeStruct((b, HEAD_PAIRS, nb, LANES, MOBA_BLOCK), BF16),
        ],
        scratch_shapes=[pltpu.VMEM((MAX_BLOCKS, width), F32)],
        compiler_params=pltpu.CompilerParams(
            dimension_semantics=("parallel", "arbitrary"), vmem_limit_bytes=VMEM_LIMIT),
        name="attn_prep",
    )(proj3, proj3, proj3, qg, kg)


def _attn_kernel(bounded_ref, q_ref, k_ref, vt_ref, sel_ref, o_ref, qm_ref, acc_ref, s_ref, p_ref):
    i = pl.program_id(2)
    heads = 2 * ATTN_PAIRS_PER_STEP
    lane = lax.broadcasted_iota(jnp.int32, (1, LANES), 1)
    is_a = lane < HEAD_DIM
    zero = jnp.zeros((), BF16)
    for pp in range(ATTN_PAIRS_PER_STEP):
        qp = q_ref[:, pp * LANES:(pp + 1) * LANES]
        qm_ref[2 * pp] = jnp.where(is_a, qp, zero)
        qm_ref[2 * pp + 1] = jnp.where(is_a, zero, qp)

    def scores(j, h):
        pp = h // 2
        kt = k_ref[pl.ds(pl.multiple_of(j * MOBA_BLOCK, MOBA_BLOCK), MOBA_BLOCK), pp * LANES:(pp + 1) * LANES]
        return _nt_dot(kt, qm_ref[h])

    def values(j, h):
        return vt_ref[h // 2, j, (h % 2) * HEAD_DIM:(h % 2 + 1) * HEAD_DIM, :]

    krow = lax.broadcasted_iota(jnp.int32, (MOBA_BLOCK, MOBA_BLOCK), 0)
    qcol = lax.broadcasted_iota(jnp.int32, (MOBA_BLOCK, MOBA_BLOCK), 1)
    causal = krow <= qcol
    zero_row = jnp.zeros((1, MOBA_BLOCK), F32)

    def bounded_scores():
        ls = [zero_row] * heads
        for h in range(heads):
            p = jnp.exp2(jnp.where(causal, scores(i, h), NEG))
            ls[h] = jnp.sum(p, axis=0, keepdims=True)
            p_ref[0, h] = p.astype(BF16)
        for h in range(heads):
            acc_ref[h] = jnp.dot(values(i, h), p_ref[0, h], preferred_element_type=F32)

        def body(jj, ls):
            ls = list(ls)
            for u in range(2):
                j = 2 * jj + u
                for h in range(heads):
                    picked = sel_ref[h, pl.ds(j, 1), :] > 0.0
                    if u == 1:
                        picked = jnp.logical_and(picked, j < i)
                    p = jnp.exp2(scores(j, h) + jnp.where(picked, 0.0, NEG))
                    ls[h] = ls[h] + jnp.sum(p, axis=0, keepdims=True)
                    p_ref[u, h] = p.astype(BF16)
            for u in range(2):
                for h in range(heads):
                    acc_ref[h] += jnp.dot(values(2 * jj + u, h), p_ref[u, h], preferred_element_type=F32)
            return tuple(ls)

        return lax.fori_loop(0, (i + 1) // 2, body, tuple(ls))

    def running_max():
        m0, l0 = [], []
        for h in range(heads):
            s = jnp.where(causal, scores(i, h), NEG)
            m0.append(jnp.max(s, axis=0, keepdims=True))
            s_ref[h] = s
        for h in range(heads):
            p = jnp.exp2(s_ref[h] - m0[h])
            l0.append(jnp.sum(p, axis=0, keepdims=True))
            p_ref[0, h] = p.astype(BF16)
        for h in range(heads):
            acc_ref[h] = jnp.dot(values(i, h), p_ref[0, h], preferred_element_type=F32)

        def body(j, carry):
            ms, ls = carry
            tile_max, new_m, new_l, alphas = [], [], [], []
            for h in range(heads):
                s = scores(j, h)
                tile_max.append(jnp.max(s, axis=0, keepdims=True))
                s_ref[h] = s
            for h in range(heads):
                picked = sel_ref[h, pl.ds(j, 1), :] > 0.0
                m = jnp.maximum(ms[h], jnp.where(picked, tile_max[h], NEG))
                p = jnp.exp2(s_ref[h] - jnp.where(picked, m, -NEG))
                alpha = jnp.exp2(ms[h] - m)
                new_m.append(m)
                new_l.append(alpha * ls[h] + jnp.sum(p, axis=0, keepdims=True))
                alphas.append(alpha)
                p_ref[0, h] = p.astype(BF16)
            for h in range(heads):
                acc_ref[h] = alphas[h] * acc_ref[h] + jnp.dot(values(j, h), p_ref[0, h],
                                                              preferred_element_type=F32)
            return tuple(new_m), tuple(new_l)

        return lax.fori_loop(0, i, body, (tuple(m0), tuple(l0)))[1]

    ls = lax.cond(bounded_ref[0] != 0, bounded_scores, running_max)
    for pp in range(ATTN_PAIRS_PER_STEP):
        out_t = jnp.concatenate([acc_ref[2 * pp] * (1.0 / ls[2 * pp]),
                                 acc_ref[2 * pp + 1] * (1.0 / ls[2 * pp + 1])], axis=0)
        o_ref[:, pp * LANES:(pp + 1) * LANES] = out_t.T.astype(o_ref.dtype)


def _moba_attention(bounded, qn, kn, sel, vt5):
    b, s, width = qn.shape
    nb = s // MOBA_BLOCK
    pps = ATTN_PAIRS_PER_STEP
    gw = pps * LANES
    qtile = pl.BlockSpec((None, MOBA_BLOCK, gw), lambda bb, g, i: (bb, i, g))
    return pl.pallas_call(
        _attn_kernel,
        grid=(b, HEAD_PAIRS // pps, nb),
        in_specs=[
            pl.BlockSpec(memory_space=pltpu.SMEM),
            qtile,
            pl.BlockSpec((None, s, gw), lambda bb, g, i: (bb, 0, g), pipeline_mode=pl.Buffered(1)),
            pl.BlockSpec((None, pps, nb, LANES, MOBA_BLOCK), lambda bb, g, i: (bb, g, 0, 0, 0),
                         pipeline_mode=pl.Buffered(1)),
            pl.BlockSpec((None, 2 * pps, None, MAX_BLOCKS, MOBA_BLOCK), lambda bb, g, i: (bb, g, i, 0, 0)),
        ],
        out_specs=qtile,
        out_shape=jax.ShapeDtypeStruct((b, s, width), BF16),
        scratch_shapes=[
            pltpu.VMEM((2 * pps, MOBA_BLOCK, LANES), BF16),
            pltpu.VMEM((2 * pps, HEAD_DIM, MOBA_BLOCK), F32),
            pltpu.VMEM((2 * pps, MOBA_BLOCK, MOBA_BLOCK), F32),
            pltpu.VMEM((2, 2 * pps, MOBA_BLOCK, MOBA_BLOCK), BF16),
        ],
        compiler_params=pltpu.CompilerParams(
            dimension_semantics=("parallel", "parallel", "arbitrary"), vmem_limit_bytes=VMEM_LIMIT),
        name="moba_attn",
    )(bounded, qn, kn, vt5, sel)


def _ssd_kernel(xbc_ref, z_ref, dt_ref, cw_ref, cb_ref, dtb_ref, alog_ref, dsk_ref, ng_ref, ec_ref, ew_ref,
                o_ref, ext_ref, act_ref, state_ref, y_ref, acst_ref, ldtt_ref, rhs_ref, ly_ref, ls_ref):
    c = pl.program_id(1)
    L = SSM_CHUNK
    n_blk = act_ref.shape[0]
    b_blk = SSM_PAIRS
    c_blk = SSM_PAIRS + SSM_GROUPS

    @pl.when(c == 0)
    def _():
        ext_ref[:, 0:CONV_PAD, :] = jnp.zeros((n_blk, CONV_PAD, LANES), F32)
        state_ref[...] = jnp.zeros_like(state_ref)

    def conv_silu(blk):
        ext_ref[blk, CONV_PAD:CONV_PAD + L, :] = xbc_ref[:, blk * LANES:(blk + 1) * LANES].astype(F32)
        w = cw_ref[blk]
        conv = cb_ref[blk]
        for k in range(SSM_CONV):
            start = CONV_PAD - (SSM_CONV - 1) + k
            conv = conv + w[k:k + 1, :] * ext_ref[blk, start:start + L, :]
        ext_ref[blk, 0:CONV_PAD, :] = ext_ref[blk, L:L + CONV_PAD, :]
        act_ref[blk] = conv * _sigmoid(conv)

    dt_in = dt_ref[...] + dtb_ref[...]
    dtv = jnp.maximum(dt_in, 0.0) + jnp.log(1.0 + jnp.exp(-jnp.abs(dt_in)))
    a2 = dtv * (-LOG2E * jnp.exp(alog_ref[...]))
    row = lax.broadcasted_iota(jnp.int32, (L, L), 0)
    col = lax.broadcasted_iota(jnp.int32, (L, L), 1)
    tri = col <= row
    acs = jnp.dot(tri.astype(F32), a2, preferred_element_type=F32, precision=lax.Precision.HIGHEST)
    acst_ref[...] = acs.T
    ldtt_ref[...] = (jnp.log(dtv) * LOG2E).T
    lane = lax.broadcasted_iota(jnp.int32, (1, LANES), 1)
    is_a = lane < HEAD_DIM
    zero = jnp.zeros((), BF16)

    def split3(v):
        hi = v.astype(BF16)
        r1 = v - hi.astype(F32)
        mid = r1.astype(BF16)
        lo = (r1 - mid.astype(F32)).astype(BF16)
        return jnp.where(lane < SSM_HEADS, hi, jnp.where(lane < 2 * SSM_HEADS, mid, lo))

    acs_parts = split3(acs)
    wst_parts = split3(jnp.exp2(acs))
    ppg = SSM_PAIRS // SSM_GROUPS

    for g in range(SSM_GROUPS):
        for blk in list(range(g * ppg, (g + 1) * ppg)) + [b_blk + g, c_blk + g]:
            conv_silu(blk)
        bg = act_ref[b_blk + g]
        cbm = _nt_dot(act_ref[c_blk + g].astype(BF16), bg.astype(BF16))
        bt = bg.T
        cmat = jnp.dot(acs_parts, ec_ref[g], preferred_element_type=F32)
        wst = jnp.dot(wst_parts, ew_ref[g], preferred_element_type=F32)
        prev = jnp.concatenate([state_ref[g * ppg + pj] for pj in range(ppg)], axis=1).astype(BF16)
        y_off = jnp.dot(act_ref[c_blk + g].astype(BF16), prev, preferred_element_type=F32) * wst
        decs = []
        for pj in range(ppg):
            xp = act_ref[g * ppg + pj].astype(BF16)
            rhs_ref[g * ppg + pj] = jnp.concatenate([jnp.where(is_a, xp, zero), jnp.where(is_a, zero, xp)], axis=0)
            lhs_y, lhs_s, dec = [], [], []
            for hh in range(2):
                k = 2 * pj + hh
                h = g * 2 * ppg + k
                arow = acst_ref[h:h + 1, :]
                rrow = arow - ldtt_ref[h:h + 1, :]
                expo = jnp.where(tri, cmat[:, k * LANES:(k + 1) * LANES] - rrow, NEG)
                lhs_y.append((jnp.exp2(expo) * cbm).astype(BF16))
                tot = arow[:, L - 1:L]
                lhs_s.append((bt * jnp.exp2(tot - rrow)).astype(BF16))
                dec.append(jnp.exp2(tot))
            ly_ref[g * ppg + pj] = jnp.concatenate(lhs_y, axis=1)
            ls_ref[g * ppg + pj] = jnp.concatenate(lhs_s, axis=1)
            decs.append(jnp.where(is_a, dec[0], dec[1]))
        for pj in range(ppg):
            y_ref[g * ppg + pj] = (jnp.dot(ly_ref[g * ppg + pj], rhs_ref[g * ppg + pj], preferred_element_type=F32)
                                   + y_off[:, pj * LANES:(pj + 1) * LANES])
        for pj in range(ppg):
            j = g * ppg + pj
            state_ref[j] = state_ref[j] * decs[pj] + jnp.dot(ls_ref[g * ppg + pj], rhs_ref[g * ppg + pj],
                                                            preferred_element_type=F32)

        ys = []
        ssq = jnp.zeros((L, 1), F32)
        for j in range(g * ppg, (g + 1) * ppg):
            cols = slice(j * LANES, (j + 1) * LANES)
            zf = z_ref[:, cols].astype(F32)
            y = (y_ref[j] + dsk_ref[:, cols] * act_ref[j]) * (zf * _sigmoid(zf))
            ssq = ssq + jnp.sum(y * y, axis=-1, keepdims=True)
            ys.append(y)
        scale = lax.rsqrt(ssq * (1.0 / (ppg * LANES)) + EPS)
        for j, y in zip(range(g * ppg, (g + 1) * ppg), ys):
            cols = slice(j * LANES, (j + 1) * LANES)
            o_ref[:, cols] = (y * scale * ng_ref[:, cols]).astype(o_ref.dtype)


def _head_expansion(width):
    r = lax.broadcasted_iota(jnp.int32, (LANES, SSM_HEADS * width), 0)
    col = lax.broadcasted_iota(jnp.int32, (LANES, SSM_HEADS * width), 1)
    e = jnp.logical_and(r < 3 * SSM_HEADS, r % SSM_HEADS == col // width).astype(BF16)
    return e.reshape(LANES, SSM_GROUPS, -1).transpose(1, 0, 2)


def _ssd(proj3, dt3, conv_w, conv_b, dt_bias, a_log, d_skip_x, norm_g):
    b, s, _ = proj3.shape
    L = SSM_CHUNK
    inner = SSM_HEADS * HEAD_DIM
    conv_dim = inner + 2 * SSM_GROUPS * SSM_STATE
    n_blk = conv_dim // LANES
    hpg = SSM_HEADS // SSM_GROUPS
    ppg = SSM_PAIRS // SSM_GROUPS
    vec = lambda w: pl.BlockSpec((1, w), lambda bb, c: (0, 0))
    return pl.pallas_call(
        _ssd_kernel,
        grid=(b, s // L),
        in_specs=[
            pl.BlockSpec((None, L, conv_dim), lambda bb, c: (bb, c, 1)),
            pl.BlockSpec((None, L, inner), lambda bb, c: (bb, c, 3)),
            pl.BlockSpec((None, L, LANES), lambda bb, c: (bb, c, 0)),
            pl.BlockSpec((n_blk, SSM_CONV, LANES), lambda bb, c: (0, 0, 0)),
            pl.BlockSpec((n_blk, 1, LANES), lambda bb, c: (0, 0, 0)),
            vec(LANES), vec(LANES), vec(inner), vec(inner),
            pl.BlockSpec((SSM_GROUPS, LANES, hpg * L), lambda bb, c: (0, 0, 0)),
            pl.BlockSpec((SSM_GROUPS, LANES, hpg * HEAD_DIM), lambda bb, c: (0, 0, 0)),
        ],
        out_specs=pl.BlockSpec((None, L, inner), lambda bb, c: (bb, c, 0)),
        out_shape=jax.ShapeDtypeStruct((b, s, inner), BF16),
        scratch_shapes=[
            pltpu.VMEM((n_blk, L + CONV_PAD, LANES), F32),
            pltpu.VMEM((n_blk, L, LANES), F32),
            pltpu.VMEM((SSM_PAIRS, SSM_STATE, LANES), F32),
            pltpu.VMEM((SSM_PAIRS, L, LANES), F32),
            pltpu.VMEM((LANES, L), F32),
            pltpu.VMEM((LANES, L), F32),
            pltpu.VMEM((SSM_PAIRS, 2 * L, LANES), BF16),
            pltpu.VMEM((SSM_PAIRS, L, 2 * L), BF16),
            pltpu.VMEM((SSM_PAIRS, SSM_STATE, 2 * L), BF16),
        ],
        compiler_params=pltpu.CompilerParams(
            dimension_semantics=("parallel", "arbitrary"), vmem_limit_bytes=VMEM_LIMIT),
        name="ssd",
    )(proj3, proj3, dt3, conv_w, conv_b, dt_bias, a_log, d_skip_x, norm_g,
      _head_expansion(L), _head_expansion(HEAD_DIM))


def _mix_kernel(att_ref, yn_ref, ga_ref, gb_ref, x_ref, woa_ref, wos_ref, wout_ref, o_ref):
    ya = jnp.dot(att_ref[...], woa_ref[...], preferred_element_type=F32)
    yb = jnp.dot(yn_ref[...], wos_ref[...], preferred_element_type=F32)
    merged = _sigmoid(ga_ref[...].astype(F32)) * ya + _sigmoid(gb_ref[...].astype(F32)) * yb
    o_ref[...] = x_ref[...] + jnp.dot(merged.astype(BF16), wout_ref[...], preferred_element_type=F32)


def _mix_out(att2d, yn2d, proj2d, x2d, w_oa, w_os, w_out, tm=512):
    m, d = x2d.shape
    ga_col = (proj2d.shape[1] - 2 * d) // d
    row = lambda w, col=0: pl.BlockSpec((tm, w), lambda i, col=col: (i, col))
    full = lambda a: pl.BlockSpec(a.shape, lambda i: (0, 0))
    return pl.pallas_call(
        _mix_kernel,
        grid=(m // tm,),
        in_specs=[row(att2d.shape[1]), row(yn2d.shape[1]), row(d, ga_col), row(d, ga_col + 1), row(d),
                  full(w_oa), full(w_os), full(w_out)],
        out_specs=row(d),
        out_shape=jax.ShapeDtypeStruct((m, d), F32),
        compiler_params=pltpu.CompilerParams(
            dimension_semantics=("parallel",), vmem_limit_bytes=VMEM_LIMIT),
        name="mix_out",
    )(att2d, yn2d, proj2d, proj2d, x2d, w_oa, w_os, w_out)


def _rms(x, g):
    return x * lax.rsqrt(jnp.mean(x * x, axis=-1, keepdims=True) + EPS) * g


def _ffn_kernel(x_ref, g2_ref, wgu_ref, wd_ref, p_ref, g3_ref, wpg_ref, wpp_ref, o_ref):
    x = x_ref[...]
    d_ff = wd_ref.shape[0]
    h = _rms(x, g2_ref[...]).astype(BF16)
    gate = jnp.dot(h, wgu_ref[:, :d_ff], preferred_element_type=F32)
    up = jnp.dot(h, wgu_ref[:, d_ff:], preferred_element_type=F32)
    act = (gate * _sigmoid(gate) * up).astype(BF16)
    x2 = x + jnp.dot(act, wd_ref[...], preferred_element_type=F32)
    h3 = _rms(x2, g3_ref[...]).astype(BF16)
    pgate = _sigmoid(jnp.dot(h3, wpg_ref[...], preferred_element_type=F32))
    proj = jnp.dot(p_ref[...].astype(BF16), wpp_ref[...], preferred_element_type=F32)
    o_ref[...] = x2 + proj * pgate


def _ffn_ple(x2d, g2, w_gu, w_down, p2d, g3, w_pg, w_pp, tm=512):
    m, d = x2d.shape
    vec = lambda a: pl.BlockSpec(a.shape, lambda i: (0, 0))
    resident = lambda a: pl.BlockSpec(a.shape, lambda i: (0, 0), pipeline_mode=pl.Buffered(1))
    return pl.pallas_call(
        _ffn_kernel,
        grid=(m // tm,),
        in_specs=[
            pl.BlockSpec((tm, d), lambda i: (i, 0)),
            vec(g2), resident(w_gu), resident(w_down),
            pl.BlockSpec((tm, p2d.shape[1]), lambda i: (i, 0)),
            vec(g3), resident(w_pg), resident(w_pp),
        ],
        out_specs=pl.BlockSpec((tm, d), lambda i: (i, 0)),
        out_shape=jax.ShapeDtypeStruct((m, d), F32),
        compiler_params=pltpu.CompilerParams(
            dimension_semantics=("parallel",), vmem_limit_bytes=VMEM_LIMIT),
        name="ffn_ple",
    )(x2d, g2, w_gu, w_down, p2d, g3, w_pg, w_pp)


def _head_lanes(v):
    return jnp.pad(jnp.tile(v.astype(F32), 3), (0, LANES - 3 * SSM_HEADS)).reshape(1, LANES)


def _lane_blocks(a):
    r, w = a.shape
    return a.astype(F32).reshape(r, w // LANES, LANES).transpose(1, 0, 2)


def kernel(x, p, ln1_g, w_in, q_norm_g, k_norm_g, w_o_attn, conv_w, conv_b, dt_bias, a_log, d_skip,
           ssm_norm_g, w_o_ssm, w_out, ln2_g, w_gate_up, w_down, ln3_g, w_ple_gate, w_ple_proj):
    b, s, d = x.shape
    assert s % MOBA_BLOCK == 0 and s // MOBA_BLOCK <= MAX_BLOCKS
    attn_w = ATTN_HEADS * HEAD_DIM
    inner = SSM_HEADS * HEAD_DIM
    conv_dim = inner + 2 * SSM_GROUPS * SSM_STATE
    off_z = 3 * attn_w
    off_xbc = off_z + inner
    off_dt = off_xbc + conv_dim
    off_g = off_dt + SSM_HEADS
    row = lambda v: v.astype(F32).reshape(1, -1)

    for i in range(w_in.shape[0]):
        wi = w_in[i]
        tn = 1024
        row_starts = [r for lo, hi in ((0, off_z), (off_xbc, off_dt), (off_z, off_xbc), (off_g, wi.shape[1]))
                      for r in range(lo, hi, tn)]
        w_t = wi.T.astype(BF16)
        w_dt = jnp.pad(jnp.tile(wi[:, off_dt:off_g], (1, 3)), ((0, 0), (0, LANES - 3 * SSM_HEADS))).astype(BF16)

        x2d = x.reshape(b * s, d)
        proj, dt_raw = _in_proj(x2d, row(ln1_g[i]), w_t, row_starts, w_dt, tn=tn)
        proj3 = proj.reshape(b, s, -1)

        qn, kn, sel, vt5 = _attn_prep(
            proj3, row(jnp.tile(q_norm_g[i], ATTN_HEADS)), row(jnp.tile(k_norm_g[i], ATTN_HEADS)))
        score_bound = 1.02 * LOG2E * HEAD_DIM ** 0.5 * jnp.max(jnp.abs(q_norm_g[i])) * jnp.max(jnp.abs(k_norm_g[i]))
        bounded = (score_bound <= SAFE_EXP2_RANGE).astype(jnp.int32).reshape(1)
        att = _moba_attention(bounded, qn, kn, sel, vt5)

        yn = _ssd(proj3, dt_raw.reshape(b, s, LANES), _lane_blocks(conv_w[i]), _lane_blocks(row(conv_b[i])),
                  _head_lanes(dt_bias[i]), _head_lanes(a_log[i]), row(jnp.repeat(d_skip[i], HEAD_DIM)),
                  row(ssm_norm_g[i]))

        x1 = _mix_out(att.reshape(b * s, attn_w), yn.reshape(b * s, inner), proj, x2d,
                      w_o_attn[i].astype(BF16), w_o_ssm[i].astype(BF16), w_out[i].astype(BF16))
        x2 = _ffn_ple(x1, row(ln2_g[i]), w_gate_up[i].astype(BF16), w_down[i].astype(BF16),
                      p[i].reshape(b * s, -1), row(ln3_g[i]), w_ple_gate[i].astype(BF16),
                      w_ple_proj[i].astype(BF16))
        x = x2.reshape(b, s, d)
    return x
```

```python
import math

import jax
import jax.numpy as jnp
from jax import lax
from jax.experimental import pallas as pl
from jax.experimental.pallas import tpu as pltpu

F32 = jnp.float32
BF16 = jnp.bfloat16

EPS = 1e-6
LOG2E = math.log2(math.e)
LANES = 128
HEAD_DIM = 64
ATTN_HEADS = 16
HEAD_PAIRS = ATTN_HEADS // 2
ATTN_PAIRS_PER_STEP = 8
MOBA_BLOCK = 256
MOBA_TOPK = 3
MAX_BLOCKS = 32
SSM_HEADS = 32
SSM_PAIRS = SSM_HEADS // 2
SSM_GROUPS = 4
SSM_STATE = 128
SSM_CONV = 4
SSM_CHUNK = 128
SSM_CHUNKS_PER_STEP = 2
CONV_PAD = 8
NEG = -1e30
SAFE_EXP2_RANGE = 60.0
VMEM_LIMIT = 56 * 1024 * 1024


def _sigmoid(v):
    return 1.0 / (1.0 + jnp.exp(-v))


def _nt_dot(a, b, **kw):
    return lax.dot_general(a, b, (((1,), (1,)), ((), ())), preferred_element_type=F32, **kw)


def _inproj_kernel(x_ref, g_ref, w_ref, wdt_ref, o_ref, dt_ref, h_ref):
    @pl.when(pl.program_id(1) == 0)
    def _():
        x = x_ref[...]
        ms = jnp.mean(x * x, axis=-1, keepdims=True)
        h = (x * lax.rsqrt(ms + EPS) * g_ref[...]).astype(BF16)
        h_ref[...] = h
        dt_ref[...] = jnp.dot(h, wdt_ref[...], preferred_element_type=F32)

    o_ref[...] = _nt_dot(h_ref[...], w_ref[...]).astype(o_ref.dtype)


def _in_proj(x2d, g, w_t, row_starts, w_dt, tm=2048, tn=1024):
    m, d = x2d.shape
    n = len(row_starts) * tn

    def weight_rows(i, j):
        start = row_starts[-1]
        for k, r in enumerate(row_starts[:-1]):
            start = jnp.where(j == k, r, start)
        return pl.multiple_of(start, math.gcd(*row_starts[1:])), 0

    return pl.pallas_call(
        _inproj_kernel,
        grid=(m // tm, n // tn),
        in_specs=[
            pl.BlockSpec((tm, d), lambda i, j: (i, 0)),
            pl.BlockSpec((1, d), lambda i, j: (0, 0)),
            pl.BlockSpec((pl.Element(tn), pl.Element(d)), weight_rows),
            pl.BlockSpec((d, LANES), lambda i, j: (0, 0)),
        ],
        out_specs=[
            pl.BlockSpec((tm, tn), lambda i, j: (i, j)),
            pl.BlockSpec((tm, LANES), lambda i, j: (i, 0)),
        ],
        out_shape=[jax.ShapeDtypeStruct((m, n), BF16), jax.ShapeDtypeStruct((m, LANES), F32)],
        scratch_shapes=[pltpu.VMEM((tm, d), BF16)],
        compiler_params=pltpu.CompilerParams(
            dimension_semantics=("parallel", "arbitrary"), vmem_limit_bytes=VMEM_LIMIT),
        name="in_proj",
    )(x2d, g, w_t, w_dt)


def _head_rms_norm(x, gain, is_a):
    outs = []
    for c in range(x.shape[1] // LANES):
        blk = x[:, c * LANES:(c + 1) * LANES]
        sq = blk * blk
        sa = jnp.sum(jnp.where(is_a, sq, 0.0), axis=-1, keepdims=True)
        sb = jnp.sum(jnp.where(is_a, 0.0, sq), axis=-1, keepdims=True)
        ms = jnp.where(is_a, sa, sb) * (1.0 / HEAD_DIM)
        outs.append(blk * lax.rsqrt(ms + EPS))
    return jnp.concatenate(outs, axis=1) * gain


def _prep_kernel(q_ref, k_ref, v_ref, qg_ref, kg_ref, qn_ref, kn_ref, sel_ref, vt_ref, kmean_ref):
    i = pl.program_id(1)

    @pl.when(i == 0)
    def _():
        kmean_ref[...] = jnp.zeros_like(kmean_ref)

    lane = lax.broadcasted_iota(jnp.int32, (1, LANES), 1)
    is_a = lane < HEAD_DIM
    qn = _head_rms_norm(q_ref[...].astype(F32), qg_ref[...], is_a) * (HEAD_DIM ** -0.5 * LOG2E)
    kn = _head_rms_norm(k_ref[...].astype(F32), kg_ref[...], is_a)
    kn_ref[...] = kn.astype(BF16)

    vt = v_ref[...].astype(F32).T
    vt_ref[...] = vt.reshape(vt_ref.shape).astype(BF16)

    kmean = kmean_ref[...]
    q_hi = qn.astype(BF16)
    qn_ref[...] = q_hi
    q_lo = (qn - q_hi.astype(F32)).astype(BF16)
    km_hi = kmean.astype(BF16)
    km_lo = (kmean - km_hi.astype(F32)).astype(BF16)
    zero = jnp.zeros((), BF16)
    jidx = lax.broadcasted_iota(jnp.int32, (MAX_BLOCKS, MOBA_BLOCK), 0).astype(F32)
    valid = jidx < i.astype(F32)
    for p in range(HEAD_PAIRS):
        cols = slice(p * LANES, (p + 1) * LANES)
        km4 = jnp.concatenate([jnp.where(is_a, km_hi[:, cols], zero), jnp.where(is_a, zero, km_hi[:, cols]),
                               jnp.where(is_a, km_lo[:, cols], zero), jnp.where(is_a, zero, km_lo[:, cols])],
                              axis=0)
        g_hi = _nt_dot(km4, q_hi[:, cols])
        g_lo = _nt_dot(km4[:2 * MAX_BLOCKS], q_lo[:, cols])
        for hh in range(2):
            r0 = hh * MAX_BLOCKS
            gate = (g_hi[r0:r0 + MAX_BLOCKS] + g_hi[r0 + 2 * MAX_BLOCKS:r0 + 3 * MAX_BLOCKS]
                    + g_lo[r0:r0 + MAX_BLOCKS])
            gate = jnp.where(valid, gate, -jnp.inf)
            sel = jnp.zeros_like(gate)
            for _ in range(MOBA_TOPK):
                top = jnp.max(gate, axis=0, keepdims=True)
                first = jnp.min(jnp.where(gate == top, jidx, float(MAX_BLOCKS)), axis=0, keepdims=True)
                pick = jidx == first
                sel = jnp.where(pick, 1.0, sel)
                gate = jnp.where(pick, -jnp.inf, gate)
            sel_ref[2 * p + hh] = jnp.where(valid, sel, 0.0)

    kmean_ref[pl.ds(i, 1), :] = jnp.mean(kn, axis=0, keepdims=True)


def _attn_prep(proj3, qg, kg):
    b, s, _ = proj3.shape
    nb = s // MOBA_BLOCK
    width = ATTN_HEADS * HEAD_DIM
    tile = lambda col: pl.BlockSpec((None, MOBA_BLOCK, width), lambda bb, i, col=col: (bb, i, col))
    gain = pl.BlockSpec((1, width), lambda bb, i: (0, 0))
    return pl.pallas_call(
        _prep_kernel,
        grid=(b, nb),
        in_specs=[tile(0), tile(1), tile(2), gain, gain],
        out_specs=[
            tile(0), tile(0),
            pl.BlockSpec((None, ATTN_HEADS, None, MAX_BLOCKS, MOBA_BLOCK), lambda bb, i: (bb, 0, i, 0, 0)),
            pl.BlockSpec((None, HEAD_PAIRS, None, LANES, MOBA_BLOCK), lambda bb, i: (bb, 0, i, 0, 0)),
        ],
        out_shape=[
            jax.ShapeDtypeStruct((b, s, width), BF16),
            jax.ShapeDtypeStruct((b, s, width), BF16),
            jax.ShapeDtypeStruct((b, ATTN_HEADS, nb, MAX_BLOCKS, MOBA_BLOCK), F32),
            jax.ShapeDtypeStruct((b, HEAD_PAIRS, nb, LANES, MOBA_BLOCK), BF16),
        ],
        scratch_shapes=[pltpu.VMEM((MAX_BLOCKS, width), F32)],
        compiler_params=pltpu.CompilerParams(
            dimension_semantics=("parallel", "arbitrary"), vmem_limit_bytes=VMEM_LIMIT),
        name="attn_prep",
    )(proj3, proj3, proj3, qg, kg)


def _attn_kernel(bounded_ref, q_ref, k_ref, vt_ref, sel_ref, o_ref, qm_ref, acc_ref, s_ref, p_ref):
    i = pl.program_id(2)
    heads = 2 * ATTN_PAIRS_PER_STEP
    lane = lax.broadcasted_iota(jnp.int32, (1, LANES), 1)
    is_a = lane < HEAD_DIM
    zero = jnp.zeros((), BF16)
    for pp in range(ATTN_PAIRS_PER_STEP):
        qp = q_ref[:, pp * LANES:(pp + 1) * LANES]
        qm_ref[2 * pp] = jnp.where(is_a, qp, zero)
        qm_ref[2 * pp + 1] = jnp.where(is_a, zero, qp)

    def scores(j, h):
        pp = h // 2
        kt = k_ref[pl.ds(pl.multiple_of(j * MOBA_BLOCK, MOBA_BLOCK), MOBA_BLOCK), pp * LANES:(pp + 1) * LANES]
        return _nt_dot(kt, qm_ref[h])

    def values(j, h):
        return vt_ref[h // 2, j, (h % 2) * HEAD_DIM:(h % 2 + 1) * HEAD_DIM, :]

    krow = lax.broadcasted_iota(jnp.int32, (MOBA_BLOCK, MOBA_BLOCK), 0)
    qcol = lax.broadcasted_iota(jnp.int32, (MOBA_BLOCK, MOBA_BLOCK), 1)
    causal = krow <= qcol
    zero_row = jnp.zeros((1, MOBA_BLOCK), F32)

    def bounded_scores():
        ls = [zero_row] * heads
        for h in range(heads):
            p = jnp.exp2(jnp.where(causal, scores(i, h), NEG))
            ls[h] = jnp.sum(p, axis=0, keepdims=True)
            p_ref[0, h] = p.astype(BF16)
        for h in range(heads):
            acc_ref[h] = jnp.dot(values(i, h), p_ref[0, h], preferred_element_type=F32)

        def body(jj, ls):
            ls = list(ls)
            for u in range(2):
                j = 2 * jj + u
                for h in range(heads):
                    picked = sel_ref[h, pl.ds(j, 1), :] > 0.0
                    if u == 1:
                        picked = jnp.logical_and(picked, j < i)
                    p = jnp.exp2(scores(j, h) + jnp.where(picked, 0.0, NEG))
                    ls[h] = ls[h] + jnp.sum(p, axis=0, keepdims=True)
                    p_ref[u, h] = p.astype(BF16)
            for u in range(2):
                for h in range(heads):
                    acc_ref[h] += jnp.dot(values(2 * jj + u, h), p_ref[u, h], preferred_element_type=F32)
            return tuple(ls)

        return lax.fori_loop(0, (i + 1) // 2, body, tuple(ls))

    def running_max():
        m0, l0 = [], []
        for h in range(heads):
            s = jnp.where(causal, scores(i, h), NEG)
            m0.append(jnp.max(s, axis=0, keepdims=True))
            s_ref[h] = s
        for h in range(heads):
            p = jnp.exp2(s_ref[h] - m0[h])
            l0.append(jnp.sum(p, axis=0, keepdims=True))
            p_ref[0, h] = p.astype(BF16)
        for h in range(heads):
            acc_ref[h] = jnp.dot(values(i, h), p_ref[0, h], preferred_element_type=F32)

        def body(j, carry):
            ms, ls = carry
            tile_max, new_m, new_l, alphas = [], [], [], []
            for h in range(heads):
                s = scores(j, h)
                tile_max.append(jnp.max(s, axis=0, keepdims=True))
                s_ref[h] = s
            for h in range(heads):
                picked = sel_ref[h, pl.ds(j, 1), :] > 0.0
                m = jnp.maximum(ms[h], jnp.where(picked, tile_max[h], NEG))
                p = jnp.exp2(s_ref[h] - jnp.where(picked, m, -NEG))
                alpha = jnp.exp2(ms[h] - m)
                new_m.append(m)
                new_l.append(alpha * ls[h] + jnp.sum(p, axis=0, keepdims=True))
                alphas.append(alpha)
                p_ref[0, h] = p.astype(BF16)
            for h in range(heads):
                acc_ref[h] = alphas[h] * acc_ref[h] + jnp.dot(values(j, h), p_ref[0, h],
                                                              preferred_element_type=F32)
            return tuple(new_m), tuple(new_l)

        return lax.fori_loop(0, i, body, (tuple(m0), tuple(l0)))[1]

    ls = lax.cond(bounded_ref[0] != 0, bounded_scores, running_max)
    for pp in range(ATTN_PAIRS_PER_STEP):
        out_t = jnp.concatenate([acc_ref[2 * pp] * (1.0 / ls[2 * pp]),
                                 acc_ref[2 * pp + 1] * (1.0 / ls[2 * pp + 1])], axis=0)
        o_ref[:, pp * LANES:(pp + 1) * LANES] = out_t.T.astype(o_ref.dtype)


def _moba_attention(bounded, qn, kn, sel, vt5):
    b, s, width = qn.shape
    nb = s // MOBA_BLOCK
    pps = ATTN_PAIRS_PER_STEP
    gw = pps * LANES
    qtile = pl.BlockSpec((None, MOBA_BLOCK, gw), lambda bb, g, i: (bb, i, g))
    return pl.pallas_call(
        _attn_kernel,
        grid=(b, HEAD_PAIRS // pps, nb),
        in_specs=[
            pl.BlockSpec(memory_space=pltpu.SMEM),
            qtile,
            pl.BlockSpec((None, s, gw), lambda bb, g, i: (bb, 0, g), pipeline_mode=pl.Buffered(1)),
            pl.BlockSpec((None, pps, nb, LANES, MOBA_BLOCK), lambda bb, g, i: (bb, g, 0, 0, 0),
                         pipeline_mode=pl.Buffered(1)),
            pl.BlockSpec((None, 2 * pps, None, MAX_BLOCKS, MOBA_BLOCK), lambda bb, g, i: (bb, g, i, 0, 0)),
        ],
        out_specs=qtile,
        out_shape=jax.ShapeDtypeStruct((b, s, width), BF16),
        scratch_shapes=[
            pltpu.VMEM((2 * pps, MOBA_BLOCK, LANES), BF16),
            pltpu.VMEM((2 * pps, HEAD_DIM, MOBA_BLOCK), F32),
            pltpu.VMEM((2 * pps, MOBA_BLOCK, MOBA_BLOCK), F32),
            pltpu.VMEM((2, 2 * pps, MOBA_BLOCK, MOBA_BLOCK), BF16),
        ],
        compiler_params=pltpu.CompilerParams(
            dimension_semantics=("parallel", "parallel", "arbitrary"), vmem_limit_bytes=VMEM_LIMIT),
        name="moba_attn",
    )(bounded, qn, kn, vt5, sel)


def _ssd_kernel(xbc_ref, z_ref, dt_ref, cw_ref, cb_ref, dtb_ref, alog_ref, dsk_ref, ng_ref, ec_ref, ew_ref,
                o_ref, ext_ref, act_ref, state_ref, *chunk_scratch):
    @pl.when(pl.program_id(1) == 0)
    def _():
        ext_ref[:, 0:CONV_PAD, :] = jnp.zeros((ext_ref.shape[0], CONV_PAD, LANES), F32)
        state_ref[...] = jnp.zeros_like(state_ref)

    for sub in range(SSM_CHUNKS_PER_STEP):
        rows = pl.ds(sub * SSM_CHUNK, SSM_CHUNK)
        _ssd_chunk(xbc_ref.at[rows], z_ref.at[rows], dt_ref.at[rows], cw_ref, cb_ref, dtb_ref, alog_ref, dsk_ref,
                   ng_ref, ec_ref, ew_ref, o_ref.at[rows], ext_ref, act_ref, state_ref, *chunk_scratch)


def _ssd_chunk(xbc_ref, z_ref, dt_ref, cw_ref, cb_ref, dtb_ref, alog_ref, dsk_ref, ng_ref, ec_ref, ew_ref,
               o_ref, ext_ref, act_ref, state_ref, y_ref, acst_ref, ldtt_ref, rhs_ref, ly_ref, ls_ref):
    L = SSM_CHUNK
    b_blk = SSM_PAIRS
    c_blk = SSM_PAIRS + SSM_GROUPS

    def conv_silu(blk):
        ext_ref[blk, CONV_PAD:CONV_PAD + L, :] = xbc_ref[:, blk * LANES:(blk + 1) * LANES].astype(F32)
        w = cw_ref[blk]
        conv = cb_ref[blk]
        for k in range(SSM_CONV):
            start = CONV_PAD - (SSM_CONV - 1) + k
            conv = conv + w[k:k + 1, :] * ext_ref[blk, start:start + L, :]
        ext_ref[blk, 0:CONV_PAD, :] = ext_ref[blk, L:L + CONV_PAD, :]
        act_ref[blk] = conv * _sigmoid(conv)

    dt_in = dt_ref[...] + dtb_ref[...]
    dtv = jnp.maximum(dt_in, 0.0) + jnp.log(1.0 + jnp.exp(-jnp.abs(dt_in)))
    a2 = dtv * (-LOG2E * jnp.exp(alog_ref[...]))
    row = lax.broadcasted_iota(jnp.int32, (L, L), 0)
    col = lax.broadcasted_iota(jnp.int32, (L, L), 1)
    tri = col <= row
    acs = jnp.dot(tri.astype(F32), a2, preferred_element_type=F32, precision=lax.Precision.HIGHEST)
    acst_ref[...] = acs.T
    ldtt_ref[...] = (jnp.log(dtv) * LOG2E).T
    lane = lax.broadcasted_iota(jnp.int32, (1, LANES), 1)
    is_a = lane < HEAD_DIM
    zero = jnp.zeros((), BF16)

    def split3(v):
        hi = v.astype(BF16)
        r1 = v - hi.astype(F32)
        mid = r1.astype(BF16)
        lo = (r1 - mid.astype(F32)).astype(BF16)
        return jnp.where(lane < SSM_HEADS, hi, jnp.where(lane < 2 * SSM_HEADS, mid, lo))

    acs_parts = split3(acs)
    wst_parts = split3(jnp.exp2(acs))
    ppg = SSM_PAIRS // SSM_GROUPS

    for g in range(SSM_GROUPS):
        for blk in list(range(g * ppg, (g + 1) * ppg)) + [b_blk + g, c_blk + g]:
            conv_silu(blk)
        bg = act_ref[b_blk + g]
        cbm = _nt_dot(act_ref[c_blk + g].astype(BF16), bg.astype(BF16))
        bt = bg.T
        cmat = jnp.dot(acs_parts, ec_ref[g], preferred_element_type=F32)
        wst = jnp.dot(wst_parts, ew_ref[g], preferred_element_type=F32)
        prev = jnp.concatenate([state_ref[g * ppg + pj] for pj in range(ppg)], axis=1).astype(BF16)
        y_off = jnp.dot(act_ref[c_blk + g].astype(BF16), prev, preferred_element_type=F32) * wst
        decs = []
        for pj in range(ppg):
            xp = act_ref[g * ppg + pj].astype(BF16)
            rhs_ref[g * ppg + pj] = jnp.concatenate([jnp.where(is_a, xp, zero), jnp.where(is_a, zero, xp)], axis=0)
            lhs_y, lhs_s, dec = [], [], []
            for hh in range(2):
                k = 2 * pj + hh
                h = g * 2 * ppg + k
                arow = acst_ref[h:h + 1, :]
                rrow = arow - ldtt_ref[h:h + 1, :]
                expo = jnp.where(tri, cmat[:, k * LANES:(k + 1) * LANES] - rrow, NEG)
                lhs_y.append((jnp.exp2(expo) * cbm).astype(BF16))
                tot = arow[:, L - 1:L]
                lhs_s.append((bt * jnp.exp2(tot - rrow)).astype(BF16))
                dec.append(jnp.exp2(tot))
            ly_ref[g * ppg + pj] = jnp.concatenate(lhs_y, axis=1)
            ls_ref[g * ppg + pj] = jnp.concatenate(lhs_s, axis=1)
            decs.append(jnp.where(is_a, dec[0], dec[1]))
        for pj in range(ppg):
            y_ref[g * ppg + pj] = (jnp.dot(ly_ref[g * ppg + pj], rhs_ref[g * ppg + pj], preferred_element_type=F32)
                                   + y_off[:, pj * LANES:(pj + 1) * LANES])
        for pj in range(ppg):
            j = g * ppg + pj
            state_ref[j] = state_ref[j] * decs[pj] + jnp.dot(ls_ref[g * ppg + pj], rhs_ref[g * ppg + pj],
                                                            preferred_element_type=F32)

        ys = []
        ssq = jnp.zeros((L, 1), F32)
        for j in range(g * ppg, (g + 1) * ppg):
            cols = slice(j * LANES, (j + 1) * LANES)
            zf = z_ref[:, cols].astype(F32)
            y = (y_ref[j] + dsk_ref[:, cols] * act_ref[j]) * (zf * _sigmoid(zf))
            ssq = ssq + jnp.sum(y * y, axis=-1, keepdims=True)
            ys.append(y)
        scale = lax.rsqrt(ssq * (1.0 / (ppg * LANES)) + EPS)
        for j, y in zip(range(g * ppg, (g + 1) * ppg), ys):
            cols = slice(j * LANES, (j + 1) * LANES)
            o_ref[:, cols] = (y * scale * ng_ref[:, cols]).astype(o_ref.dtype)


def _head_expansion(width):
    r = lax.broadcasted_iota(jnp.int32, (LANES, SSM_HEADS * width), 0)
    col = lax.broadcasted_iota(jnp.int32, (LANES, SSM_HEADS * width), 1)
    e = jnp.logical_and(r < 3 * SSM_HEADS, r % SSM_HEADS == col // width).astype(BF16)
    return e.reshape(LANES, SSM_GROUPS, -1).transpose(1, 0, 2)


def _ssd(proj3, dt3, conv_w, conv_b, dt_bias, a_log, d_skip_x, norm_g):
    b, s, _ = proj3.shape
    L = SSM_CHUNK
    inner = SSM_HEADS * HEAD_DIM
    conv_dim = inner + 2 * SSM_GROUPS * SSM_STATE
    n_blk = conv_dim // LANES
    hpg = SSM_HEADS // SSM_GROUPS
    ppg = SSM_PAIRS // SSM_GROUPS
    rows = SSM_CHUNKS_PER_STEP * L
    vec = lambda w: pl.BlockSpec((1, w), lambda bb, c: (0, 0))
    return pl.pallas_call(
        _ssd_kernel,
        grid=(b, s // rows),
        in_specs=[
            pl.BlockSpec((None, rows, conv_dim), lambda bb, c: (bb, c, 1)),
            pl.BlockSpec((None, rows, inner), lambda bb, c: (bb, c, 3)),
            pl.BlockSpec((None, rows, LANES), lambda bb, c: (bb, c, 0)),
            pl.BlockSpec((n_blk, SSM_CONV, LANES), lambda bb, c: (0, 0, 0)),
            pl.BlockSpec((n_blk, 1, LANES), lambda bb, c: (0, 0, 0)),
            vec(LANES), vec(LANES), vec(inner), vec(inner),
            pl.BlockSpec((SSM_GROUPS, LANES, hpg * L), lambda bb, c: (0, 0, 0)),
            pl.BlockSpec((SSM_GROUPS, LANES, hpg * HEAD_DIM), lambda bb, c: (0, 0, 0)),
        ],
        out_specs=pl.BlockSpec((None, rows, inner), lambda bb, c: (bb, c, 0)),
        out_shape=jax.ShapeDtypeStruct((b, s, inner), BF16),
        scratch_shapes=[
            pltpu.VMEM((n_blk, L + CONV_PAD, LANES), F32),
            pltpu.VMEM((n_blk, L, LANES), F32),
            pltpu.VMEM((SSM_PAIRS, SSM_STATE, LANES), F32),
            pltpu.VMEM((SSM_PAIRS, L, LANES), F32),
            pltpu.VMEM((LANES, L), F32),
            pltpu.VMEM((LANES, L), F32),
            pltpu.VMEM((SSM_PAIRS, 2 * L, LANES), BF16),
            pltpu.VMEM((SSM_PAIRS, L, 2 * L), BF16),
            pltpu.VMEM((SSM_PAIRS, SSM_STATE, 2 * L), BF16),
        ],
        compiler_params=pltpu.CompilerParams(
            dimension_semantics=("parallel", "arbitrary"), vmem_limit_bytes=VMEM_LIMIT),
        name="ssd",
    )(proj3, proj3, dt3, conv_w, conv_b, dt_bias, a_log, d_skip_x, norm_g,
      _head_expansion(L), _head_expansion(HEAD_DIM))


def _mix_kernel(att_ref, yn_ref, ga_ref, gb_ref, x_ref, woa_ref, wos_ref, wout_ref, o_ref):
    ya = jnp.dot(att_ref[...], woa_ref[...], preferred_element_type=F32)
    yb = jnp.dot(yn_ref[...], wos_ref[...], preferred_element_type=F32)
    merged = _sigmoid(ga_ref[...].astype(F32)) * ya + _sigmoid(gb_ref[...].astype(F32)) * yb
    o_ref[...] = x_ref[...] + jnp.dot(merged.astype(BF16), wout_ref[...], preferred_element_type=F32)


def _mix_out(att2d, yn2d, proj2d, x2d, w_oa, w_os, w_out, tm=512):
    m, d = x2d.shape
    ga_col = (proj2d.shape[1] - 2 * d) // d
    row = lambda w, col=0: pl.BlockSpec((tm, w), lambda i, col=col: (i, col))
    full = lambda a: pl.BlockSpec(a.shape, lambda i: (0, 0))
    return pl.pallas_call(
        _mix_kernel,
        grid=(m // tm,),
        in_specs=[row(att2d.shape[1]), row(yn2d.shape[1]), row(d, ga_col), row(d, ga_col + 1), row(d),
                  full(w_oa), full(w_os), full(w_out)],
        out_specs=row(d),
        out_shape=jax.ShapeDtypeStruct((m, d), F32),
        compiler_params=pltpu.CompilerParams(
            dimension_semantics=("parallel",), vmem_limit_bytes=VMEM_LIMIT),
        name="mix_out",
    )(att2d, yn2d, proj2d, proj2d, x2d, w_oa, w_os, w_out)


def _rms(x, g):
    return x * lax.rsqrt(jnp.mean(x * x, axis=-1, keepdims=True) + EPS) * g


def _ffn_kernel(x_ref, g2_ref, wgu_ref, wd_ref, p_ref, g3_ref, wpg_ref, wpp_ref, o_ref):
    x = x_ref[...]
    d_ff = wd_ref.shape[0]
    h = _rms(x, g2_ref[...]).astype(BF16)
    gate = jnp.dot(h, wgu_ref[:, :d_ff], preferred_element_type=F32)
    up = jnp.dot(h, wgu_ref[:, d_ff:], preferred_element_type=F32)
    act = (gate * _sigmoid(gate) * up).astype(BF16)
    x2 = x + jnp.dot(act, wd_ref[...], preferred_element_type=F32)
    h3 = _rms(x2, g3_ref[...]).astype(BF16)
    pgate = _sigmoid(jnp.dot(h3, wpg_ref[...], preferred_element_type=F32))
    proj = jnp.dot(p_ref[...].astype(BF16), wpp_ref[...], preferred_element_type=F32)
    o_ref[...] = x2 + proj * pgate


def _ffn_ple(x2d, g2, w_gu, w_down, p2d, g3, w_pg, w_pp, tm=512):
    m, d = x2d.shape
    vec = lambda a: pl.BlockSpec(a.shape, lambda i: (0, 0))
    resident = lambda a: pl.BlockSpec(a.shape, lambda i: (0, 0), pipeline_mode=pl.Buffered(1))
    return pl.pallas_call(
        _ffn_kernel,
        grid=(m // tm,),
        in_specs=[
            pl.BlockSpec((tm, d), lambda i: (i, 0)),
            vec(g2), resident(w_gu), resident(w_down),
            pl.BlockSpec((tm, p2d.shape[1]), lambda i: (i, 0)),
            vec(g3), resident(w_pg), resident(w_pp),
        ],
        out_specs=pl.BlockSpec((tm, d), lambda i: (i, 0)),
        out_shape=jax.ShapeDtypeStruct((m, d), F32),
        compiler_params=pltpu.CompilerParams(
            dimension_semantics=("parallel",), vmem_limit_bytes=VMEM_LIMIT),
        name="ffn_ple",
    )(x2d, g2, w_gu, w_down, p2d, g3, w_pg, w_pp)


def _head_lanes(v):
    return jnp.pad(jnp.tile(v.astype(F32), 3), (0, LANES - 3 * SSM_HEADS)).reshape(1, LANES)


def _lane_blocks(a):
    r, w = a.shape
    return a.astype(F32).reshape(r, w // LANES, LANES).transpose(1, 0, 2)


def kernel(x, p, ln1_g, w_in, q_norm_g, k_norm_g, w_o_attn, conv_w, conv_b, dt_bias, a_log, d_skip,
           ssm_norm_g, w_o_ssm, w_out, ln2_g, w_gate_up, w_down, ln3_g, w_ple_gate, w_ple_proj):
    b, s, d = x.shape
    assert s % MOBA_BLOCK == 0 and s // MOBA_BLOCK <= MAX_BLOCKS
    attn_w = ATTN_HEADS * HEAD_DIM
    inner = SSM_HEADS * HEAD_DIM
    conv_dim = inner + 2 * SSM_GROUPS * SSM_STATE
    off_z = 3 * attn_w
    off_xbc = off_z + inner
    off_dt = off_xbc + conv_dim
    off_g = off_dt + SSM_HEADS
    row = lambda v: v.astype(F32).reshape(1, -1)

    for i in range(w_in.shape[0]):
        wi = w_in[i]
        tn = 1024
        row_starts = [r for lo, hi in ((0, off_z), (off_xbc, off_dt), (off_z, off_xbc), (off_g, wi.shape[1]))
                      for r in range(lo, hi, tn)]
        w_t = wi.T.astype(BF16)
        w_dt = jnp.pad(jnp.tile(wi[:, off_dt:off_g], (1, 3)), ((0, 0), (0, LANES - 3 * SSM_HEADS))).astype(BF16)

        x2d = x.reshape(b * s, d)
        proj, dt_raw = _in_proj(x2d, row(ln1_g[i]), w_t, row_starts, w_dt, tn=tn)
        proj3 = proj.reshape(b, s, -1)

        qn, kn, sel, vt5 = _attn_prep(
            proj3, row(jnp.tile(q_norm_g[i], ATTN_HEADS)), row(jnp.tile(k_norm_g[i], ATTN_HEADS)))
        score_bound = 1.02 * LOG2E * HEAD_DIM ** 0.5 * jnp.max(jnp.abs(q_norm_g[i])) * jnp.max(jnp.abs(k_norm_g[i]))
        bounded = (score_bound <= SAFE_EXP2_RANGE).astype(jnp.int32).reshape(1)
        att = _moba_attention(bounded, qn, kn, sel, vt5)

        yn = _ssd(proj3, dt_raw.reshape(b, s, LANES), _lane_blocks(conv_w[i]), _lane_blocks(row(conv_b[i])),
                  _head_lanes(dt_bias[i]), _head_lanes(a_log[i]), row(jnp.repeat(d_skip[i], HEAD_DIM)),
                  row(ssm_norm_g[i]))

        x1 = _mix_out(att.reshape(b * s, attn_w), yn.reshape(b * s, inner), proj, x2d,
                      w_o_attn[i].astype(BF16), w_o_ssm[i].astype(BF16), w_out[i].astype(BF16))
        x2 = _ffn_ple(x1, row(ln2_g[i]), w_gate_up[i].astype(BF16), w_down[i].astype(BF16),
                      p[i].reshape(b * s, -1), row(ln3_g[i]), w_ple_gate[i].astype(BF16),
                      w_ple_proj[i].astype(BF16))
        x = x2.reshape(b, s, d)
    return x
```

```python
import math

import jax
import jax.numpy as jnp
from jax import lax
from jax.experimental import pallas as pl
from jax.experimental.pallas import tpu as pltpu

F32 = jnp.float32
BF16 = jnp.bfloat16

EPS = 1e-6
LOG2E = math.log2(math.e)
LANES = 128
HEAD_DIM = 64
ATTN_HEADS = 16
HEAD_PAIRS = ATTN_HEADS // 2
ATTN_PAIRS_PER_STEP = 8
MOBA_BLOCK = 256
MOBA_TOPK = 3
MAX_BLOCKS = 32
SSM_HEADS = 32
SSM_PAIRS = SSM_HEADS // 2
SSM_GROUPS = 4
SSM_STATE = 128
SSM_CONV = 4
SSM_CHUNK = 128
SSM_CHUNKS_PER_STEP = 4
CONV_PAD = 8
NEG = -1e30
SAFE_EXP2_RANGE = 60.0
VMEM_LIMIT = 56 * 1024 * 1024


def _sigmoid(v):
    return 1.0 / (1.0 + jnp.exp(-v))


def _nt_dot(a, b, **kw):
    return lax.dot_general(a, b, (((1,), (1,)), ((), ())), preferred_element_type=F32, **kw)


def _inproj_kernel(x_ref, g_ref, w_ref, wdt_ref, o_ref, dt_ref, h_ref):
    @pl.when(pl.program_id(1) == 0)
    def _():
        x = x_ref[...]
        ms = jnp.mean(x * x, axis=-1, keepdims=True)
        h = (x * lax.rsqrt(ms + EPS) * g_ref[...]).astype(BF16)
        h_ref[...] = h
        dt_ref[...] = jnp.dot(h, wdt_ref[...], preferred_element_type=F32)

    o_ref[...] = _nt_dot(h_ref[...], w_ref[...]).astype(o_ref.dtype)


def _in_proj(x2d, g, w_t, row_starts, w_dt, tm=2048, tn=1024):
    m, d = x2d.shape
    n = len(row_starts) * tn

    def weight_rows(i, j):
        start = row_starts[-1]
        for k, r in enumerate(row_starts[:-1]):
            start = jnp.where(j == k, r, start)
        return pl.multiple_of(start, math.gcd(*row_starts[1:])), 0

    return pl.pallas_call(
        _inproj_kernel,
        grid=(m // tm, n // tn),
        in_specs=[
            pl.BlockSpec((tm, d), lambda i, j: (i, 0)),
            pl.BlockSpec((1, d), lambda i, j: (0, 0)),
            pl.BlockSpec((pl.Element(tn), pl.Element(d)), weight_rows),
            pl.BlockSpec((d, LANES), lambda i, j: (0, 0)),
        ],
        out_specs=[
            pl.BlockSpec((tm, tn), lambda i, j: (i, j)),
            pl.BlockSpec((tm, LANES), lambda i, j: (i, 0)),
        ],
        out_shape=[jax.ShapeDtypeStruct((m, n), BF16), jax.ShapeDtypeStruct((m, LANES), F32)],
        scratch_shapes=[pltpu.VMEM((tm, d), BF16)],
        compiler_params=pltpu.CompilerParams(
            dimension_semantics=("parallel", "arbitrary"), vmem_limit_bytes=VMEM_LIMIT),
        name="in_proj",
    )(x2d, g, w_t, w_dt)


def _head_rms_norm(x, gain, is_a):
    outs = []
    for c in range(x.shape[1] // LANES):
        blk = x[:, c * LANES:(c + 1) * LANES]
        sq = blk * blk
        sa = jnp.sum(jnp.where(is_a, sq, 0.0), axis=-1, keepdims=True)
        sb = jnp.sum(jnp.where(is_a, 0.0, sq), axis=-1, keepdims=True)
        ms = jnp.where(is_a, sa, sb) * (1.0 / HEAD_DIM)
        outs.append(blk * lax.rsqrt(ms + EPS))
    return jnp.concatenate(outs, axis=1) * gain


def _prep_kernel(q_ref, k_ref, v_ref, qg_ref, kg_ref, qn_ref, kn_ref, sel_ref, vt_ref, kmean_ref):
    i = pl.program_id(1)

    @pl.when(i == 0)
    def _():
        kmean_ref[...] = jnp.zeros_like(kmean_ref)

    lane = lax.broadcasted_iota(jnp.int32, (1, LANES), 1)
    is_a = lane < HEAD_DIM
    qn = _head_rms_norm(q_ref[...].astype(F32), qg_ref[...], is_a) * (HEAD_DIM ** -0.5 * LOG2E)
    kn = _head_rms_norm(k_ref[...].astype(F32), kg_ref[...], is_a)
    kn_ref[...] = kn.astype(BF16)

    vt = v_ref[...].astype(F32).T
    vt_ref[...] = vt.reshape(vt_ref.shape).astype(BF16)

    kmean = kmean_ref[...]
    q_hi = qn.astype(BF16)
    qn_ref[...] = q_hi
    q_lo = (qn - q_hi.astype(F32)).astype(BF16)
    km_hi = kmean.astype(BF16)
    km_lo = (kmean - km_hi.astype(F32)).astype(BF16)
    zero = jnp.zeros((), BF16)
    jidx = lax.broadcasted_iota(jnp.int32, (MAX_BLOCKS, MOBA_BLOCK), 0).astype(F32)
    valid = jidx < i.astype(F32)
    for p in range(HEAD_PAIRS):
        cols = slice(p * LANES, (p + 1) * LANES)
        km4 = jnp.concatenate([jnp.where(is_a, km_hi[:, cols], zero), jnp.where(is_a, zero, km_hi[:, cols]),
                               jnp.where(is_a, km_lo[:, cols], zero), jnp.where(is_a, zero, km_lo[:, cols])],
                              axis=0)
        g_hi = _nt_dot(km4, q_hi[:, cols])
        g_lo = _nt_dot(km4[:2 * MAX_BLOCKS], q_lo[:, cols])
        for hh in range(2):
            r0 = hh * MAX_BLOCKS
            gate = (g_hi[r0:r0 + MAX_BLOCKS] + g_hi[r0 + 2 * MAX_BLOCKS:r0 + 3 * MAX_BLOCKS]
                    + g_lo[r0:r0 + MAX_BLOCKS])
            gate = jnp.where(valid, gate, -jnp.inf)
            sel = jnp.zeros_like(gate)
            for _ in range(MOBA_TOPK):
                top = jnp.max(gate, axis=0, keepdims=True)
                first = jnp.min(jnp.where(gate == top, jidx, float(MAX_BLOCKS)), axis=0, keepdims=True)
                pick = jidx == first
                sel = jnp.where(pick, 1.0, sel)
                gate = jnp.where(pick, -jnp.inf, gate)
            sel_ref[2 * p + hh] = jnp.where(valid, sel, 0.0)

    kmean_ref[pl.ds(i, 1), :] = jnp.mean(kn, axis=0, keepdims=True)


def _attn_prep(proj3, qg, kg):
    b, s, _ = proj3.shape
    nb = s // MOBA_BLOCK
    width = ATTN_HEADS * HEAD_DIM
    tile = lambda col: pl.BlockSpec((None, MOBA_BLOCK, width), lambda bb, i, col=col: (bb, i, col))
    gain = pl.BlockSpec((1, width), lambda bb, i: (0, 0))
    return pl.pallas_call(
        _prep_kernel,
        grid=(b, nb),
        in_specs=[tile(0), tile(1), tile(2), gain, gain],
        out_specs=[
            tile(0), tile(0),
            pl.BlockSpec((None, ATTN_HEADS, None, MAX_BLOCKS, MOBA_BLOCK), lambda bb, i: (bb, 0, i, 0, 0)),
            pl.BlockSpec((None, HEAD_PAIRS, None, LANES, MOBA_BLOCK), lambda bb, i: (bb, 0, i, 0, 0)),
        ],
        out_shape=[
            jax.ShapeDtypeStruct((b, s, width), BF16),
            jax.ShapeDtypeStruct((b, s, width), BF16),
            jax.ShapeDtypeStruct((b, ATTN_HEADS, nb, MAX_BLOCKS, MOBA_BLOCK), F32),
            jax.ShapeDtypeStruct((b, HEAD_PAIRS, nb, LANES, MOBA_BLOCK), BF16),
        ],
        scratch_shapes=[pltpu.VMEM((MAX_BLOCKS, width), F32)],
        compiler_params=pltpu.CompilerParams(
            dimension_semantics=("parallel", "arbitrary"), vmem_limit_bytes=VMEM_LIMIT),
        name="attn_prep",
    )(proj3, proj3, proj3, qg, kg)


def _attn_kernel(bounded_ref, q_ref, k_ref, vt_ref, sel_ref, o_ref, qm_ref, acc_ref, s_ref, p_ref):
    i = pl.program_id(2)
    heads = 2 * ATTN_PAIRS_PER_STEP
    lane = lax.broadcasted_iota(jnp.int32, (1, LANES), 1)
    is_a = lane < HEAD_DIM
    zero = jnp.zeros((), BF16)
    for pp in range(ATTN_PAIRS_PER_STEP):
        qp = q_ref[:, pp * LANES:(pp + 1) * LANES]
        qm_ref[2 * pp] = jnp.where(is_a, qp, zero)
        qm_ref[2 * pp + 1] = jnp.where(is_a, zero, qp)

    def scores(j, h):
        pp = h // 2
        kt = k_ref[pl.ds(pl.multiple_of(j * MOBA_BLOCK, MOBA_BLOCK), MOBA_BLOCK), pp * LANES:(pp + 1) * LANES]
        return _nt_dot(kt, qm_ref[h])

    def values(j, h):
        return vt_ref[h // 2, j, (h % 2) * HEAD_DIM:(h % 2 + 1) * HEAD_DIM, :]

    krow = lax.broadcasted_iota(jnp.int32, (MOBA_BLOCK, MOBA_BLOCK), 0)
    qcol = lax.broadcasted_iota(jnp.int32, (MOBA_BLOCK, MOBA_BLOCK), 1)
    causal = krow <= qcol
    zero_row = jnp.zeros((1, MOBA_BLOCK), F32)

    def bounded_scores():
        ls = [zero_row] * heads
        for h in range(heads):
            p = jnp.exp2(jnp.where(causal, scores(i, h), NEG))
            ls[h] = jnp.sum(p, axis=0, keepdims=True)
            p_ref[0, h] = p.astype(BF16)
        for h in range(heads):
            acc_ref[h] = jnp.dot(values(i, h), p_ref[0, h], preferred_element_type=F32)

        def body(jj, ls):
            ls = list(ls)
            for u in range(2):
                j = 2 * jj + u
                for h in range(heads):
                    picked = sel_ref[h, pl.ds(j, 1), :] > 0.0
                    if u == 1:
                        picked = jnp.logical_and(picked, j < i)
                    p = jnp.exp2(scores(j, h) + jnp.where(picked, 0.0, NEG))
                    ls[h] = ls[h] + jnp.sum(p, axis=0, keepdims=True)
                    p_ref[u, h] = p.astype(BF16)
            for u in range(2):
                for h in range(heads):
                    acc_ref[h] += jnp.dot(values(2 * jj + u, h), p_ref[u, h], preferred_element_type=F32)
            return tuple(ls)

        return lax.fori_loop(0, (i + 1) // 2, body, tuple(ls))

    def running_max():
        m0, l0 = [], []
        for h in range(heads):
            s = jnp.where(causal, scores(i, h), NEG)
            m0.append(jnp.max(s, axis=0, keepdims=True))
            s_ref[h] = s
        for h in range(heads):
            p = jnp.exp2(s_ref[h] - m0[h])
            l0.append(jnp.sum(p, axis=0, keepdims=True))
            p_ref[0, h] = p.astype(BF16)
        for h in range(heads):
            acc_ref[h] = jnp.dot(values(i, h), p_ref[0, h], preferred_element_type=F32)

        def body(j, carry):
            ms, ls = carry
            tile_max, new_m, new_l, alphas = [], [], [], []
            for h in range(heads):
                s = scores(j, h)
                tile_max.append(jnp.max(s, axis=0, keepdims=True))
                s_ref[h] = s
            for h in range(heads):
                picked = sel_ref[h, pl.ds(j, 1), :] > 0.0
                m = jnp.maximum(ms[h], jnp.where(picked, tile_max[h], NEG))
                p = jnp.exp2(s_ref[h] - jnp.where(picked, m, -NEG))
                alpha = jnp.exp2(ms[h] - m)
                new_m.append(m)
                new_l.append(alpha * ls[h] + jnp.sum(p, axis=0, keepdims=True))
                alphas.append(alpha)
                p_ref[0, h] = p.astype(BF16)
            for h in range(heads):
                acc_ref[h] = alphas[h] * acc_ref[h] + jnp.dot(values(j, h), p_ref[0, h],
                                                              preferred_element_type=F32)
            return tuple(new_m), tuple(new_l)

        return lax.fori_loop(0, i, body, (tuple(m0), tuple(l0)))[1]

    ls = lax.cond(bounded_ref[0] != 0, bounded_scores, running_max)
    for pp in range(ATTN_PAIRS_PER_STEP):
        out_t = jnp.concatenate([acc_ref[2 * pp] * (1.0 / ls[2 * pp]),
                                 acc_ref[2 * pp + 1] * (1.0 / ls[2 * pp + 1])], axis=0)
        o_ref[:, pp * LANES:(pp + 1) * LANES] = out_t.T.astype(o_ref.dtype)


def _moba_attention(bounded, qn, kn, sel, vt5):
    b, s, width = qn.shape
    nb = s // MOBA_BLOCK
    pps = ATTN_PAIRS_PER_STEP
    gw = pps * LANES
    qtile = pl.BlockSpec((None, MOBA_BLOCK, gw), lambda bb, g, i: (bb, i, g))
    return pl.pallas_call(
        _attn_kernel,
        grid=(b, HEAD_PAIRS // pps, nb),
        in_specs=[
            pl.BlockSpec(memory_space=pltpu.SMEM),
            qtile,
            pl.BlockSpec((None, s, gw), lambda bb, g, i: (bb, 0, g), pipeline_mode=pl.Buffered(1)),
            pl.BlockSpec((None, pps, nb, LANES, MOBA_BLOCK), lambda bb, g, i: (bb, g, 0, 0, 0),
                         pipeline_mode=pl.Buffered(1)),
            pl.BlockSpec((None, 2 * pps, None, MAX_BLOCKS, MOBA_BLOCK), lambda bb, g, i: (bb, g, i, 0, 0)),
        ],
        out_specs=qtile,
        out_shape=jax.ShapeDtypeStruct((b, s, width), BF16),
        scratch_shapes=[
            pltpu.VMEM((2 * pps, MOBA_BLOCK, LANES), BF16),
            pltpu.VMEM((2 * pps, HEAD_DIM, MOBA_BLOCK), F32),
            pltpu.VMEM((2 * pps, MOBA_BLOCK, MOBA_BLOCK), F32),
            pltpu.VMEM((2, 2 * pps, MOBA_BLOCK, MOBA_BLOCK), BF16),
        ],
        compiler_params=pltpu.CompilerParams(
            dimension_semantics=("parallel", "parallel", "arbitrary"), vmem_limit_bytes=VMEM_LIMIT),
        name="moba_attn",
    )(bounded, qn, kn, vt5, sel)


def _ssd_kernel(xbc_ref, z_ref, dt_ref, cw_ref, cb_ref, dtb_ref, alog_ref, dsk_ref, ng_ref, ec_ref, ew_ref,
                o_ref, ext_ref, act_ref, state_ref, *chunk_scratch):
    @pl.when(pl.program_id(1) == 0)
    def _():
        ext_ref[:, 0:CONV_PAD, :] = jnp.zeros((ext_ref.shape[0], CONV_PAD, LANES), F32)
        state_ref[...] = jnp.zeros_like(state_ref)

    for sub in range(SSM_CHUNKS_PER_STEP):
        rows = pl.ds(sub * SSM_CHUNK, SSM_CHUNK)
        _ssd_chunk(xbc_ref.at[rows], z_ref.at[rows], dt_ref.at[rows], cw_ref, cb_ref, dtb_ref, alog_ref, dsk_ref,
                   ng_ref, ec_ref, ew_ref, o_ref.at[rows], ext_ref, act_ref, state_ref, *chunk_scratch)


def _ssd_chunk(xbc_ref, z_ref, dt_ref, cw_ref, cb_ref, dtb_ref, alog_ref, dsk_ref, ng_ref, ec_ref, ew_ref,
               o_ref, ext_ref, act_ref, state_ref, y_ref, acst_ref, ldtt_ref, rhs_ref, ly_ref, ls_ref):
    L = SSM_CHUNK
    b_blk = SSM_PAIRS
    c_blk = SSM_PAIRS + SSM_GROUPS

    def conv_silu(blk):
        ext_ref[blk, CONV_PAD:CONV_PAD + L, :] = xbc_ref[:, blk * LANES:(blk + 1) * LANES].astype(F32)
        w = cw_ref[blk]
        conv = cb_ref[blk]
        for k in range(SSM_CONV):
            start = CONV_PAD - (SSM_CONV - 1) + k
            conv = conv + w[k:k + 1, :] * ext_ref[blk, start:start + L, :]
        ext_ref[blk, 0:CONV_PAD, :] = ext_ref[blk, L:L + CONV_PAD, :]
        act_ref[blk] = conv * _sigmoid(conv)

    dt_in = dt_ref[...] + dtb_ref[...]
    dtv = jnp.maximum(dt_in, 0.0) + jnp.log(1.0 + jnp.exp(-jnp.abs(dt_in)))
    a2 = dtv * (-LOG2E * jnp.exp(alog_ref[...]))
    row = lax.broadcasted_iota(jnp.int32, (L, L), 0)
    col = lax.broadcasted_iota(jnp.int32, (L, L), 1)
    tri = col <= row
    acs = jnp.dot(tri.astype(F32), a2, preferred_element_type=F32, precision=lax.Precision.HIGHEST)
    acst_ref[...] = acs.T
    ldtt_ref[...] = (jnp.log(dtv) * LOG2E).T
    lane = lax.broadcasted_iota(jnp.int32, (1, LANES), 1)
    is_a = lane < HEAD_DIM
    zero = jnp.zeros((), BF16)

    def split3(v):
        hi = v.astype(BF16)
        r1 = v - hi.astype(F32)
        mid = r1.astype(BF16)
        lo = (r1 - mid.astype(F32)).astype(BF16)
        return jnp.where(lane < SSM_HEADS, hi, jnp.where(lane < 2 * SSM_HEADS, mid, lo))

    acs_parts = split3(acs)
    wst_parts = split3(jnp.exp2(acs))
    ppg = SSM_PAIRS // SSM_GROUPS

    for g in range(SSM_GROUPS):
        for blk in list(range(g * ppg, (g + 1) * ppg)) + [b_blk + g, c_blk + g]:
            conv_silu(blk)
        bg = act_ref[b_blk + g]
        cbm = _nt_dot(act_ref[c_blk + g].astype(BF16), bg.astype(BF16))
        bt = bg.T
        cmat = jnp.dot(acs_parts, ec_ref[g], preferred_element_type=F32)
        wst = jnp.dot(wst_parts, ew_ref[g], preferred_element_type=F32)
        prev = jnp.concatenate([state_ref[g * ppg + pj] for pj in range(ppg)], axis=1).astype(BF16)
        y_off = jnp.dot(act_ref[c_blk + g].astype(BF16), prev, preferred_element_type=F32) * wst
        decs = []
        for pj in range(ppg):
            xp = act_ref[g * ppg + pj].astype(BF16)
            rhs_ref[g * ppg + pj] = jnp.concatenate([jnp.where(is_a, xp, zero), jnp.where(is_a, zero, xp)], axis=0)
            lhs_y, lhs_s, dec = [], [], []
            for hh in range(2):
                k = 2 * pj + hh
                h = g * 2 * ppg + k
                arow = acst_ref[h:h + 1, :]
                rrow = arow - ldtt_ref[h:h + 1, :]
                expo = jnp.where(tri, cmat[:, k * LANES:(k + 1) * LANES] - rrow, NEG)
                lhs_y.append((jnp.exp2(expo) * cbm).astype(BF16))
                tot = arow[:, L - 1:L]
                lhs_s.append((bt * jnp.exp2(tot - rrow)).astype(BF16))
                dec.append(jnp.exp2(tot))
            ly_ref[g * ppg + pj] = jnp.concatenate(lhs_y, axis=1)
            ls_ref[g * ppg + pj] = jnp.concatenate(lhs_s, axis=1)
            decs.append(jnp.where(is_a, dec[0], dec[1]))
        for pj in range(ppg):
            y_ref[g * ppg + pj] = (jnp.dot(ly_ref[g * ppg + pj], rhs_ref[g * ppg + pj], preferred_element_type=F32)
                                   + y_off[:, pj * LANES:(pj + 1) * LANES])
        for pj in range(ppg):
            j = g * ppg + pj
            state_ref[j] = state_ref[j] * decs[pj] + jnp.dot(ls_ref[g * ppg + pj], rhs_ref[g * ppg + pj],
                                                            preferred_element_type=F32)

        ys = []
        ssq = jnp.zeros((L, 1), F32)
        for j in range(g * ppg, (g + 1) * ppg):
            cols = slice(j * LANES, (j + 1) * LANES)
            zf = z_ref[:, cols].astype(F32)
            y = (y_ref[j] + dsk_ref[:, cols] * act_ref[j]) * (zf * _sigmoid(zf))
            ssq = ssq + jnp.sum(y * y, axis=-1, keepdims=True)
            ys.append(y)
        scale = lax.rsqrt(ssq * (1.0 / (ppg * LANES)) + EPS)
        for j, y in zip(range(g * ppg, (g + 1) * ppg), ys):
            cols = slice(j * LANES, (j + 1) * LANES)
            o_ref[:, cols] = (y * scale * ng_ref[:, cols]).astype(o_ref.dtype)


def _head_expansion(width):
    r = lax.broadcasted_iota(jnp.int32, (LANES, SSM_HEADS * width), 0)
    col = lax.broadcasted_iota(jnp.int32, (LANES, SSM_HEADS * width), 1)
    e = jnp.logical_and(r < 3 * SSM_HEADS, r % SSM_HEADS == col // width).astype(BF16)
    return e.reshape(LANES, SSM_GROUPS, -1).transpose(1, 0, 2)


def _ssd(proj3, dt3, conv_w, conv_b, dt_bias, a_log, d_skip_x, norm_g):
    b, s, _ = proj3.shape
    L = SSM_CHUNK
    inner = SSM_HEADS * HEAD_DIM
    conv_dim = inner + 2 * SSM_GROUPS * SSM_STATE
    n_blk = conv_dim // LANES
    hpg = SSM_HEADS // SSM_GROUPS
    ppg = SSM_PAIRS // SSM_GROUPS
    rows = SSM_CHUNKS_PER_STEP * L
    vec = lambda w: pl.BlockSpec((1, w), lambda bb, c: (0, 0))
    return pl.pallas_call(
        _ssd_kernel,
        grid=(b, s // rows),
        in_specs=[
            pl.BlockSpec((None, rows, conv_dim), lambda bb, c: (bb, c, 1)),
            pl.BlockSpec((None, rows, inner), lambda bb, c: (bb, c, 3)),
            pl.BlockSpec((None, rows, LANES), lambda bb, c: (bb, c, 0)),
            pl.BlockSpec((n_blk, SSM_CONV, LANES), lambda bb, c: (0, 0, 0)),
            pl.BlockSpec((n_blk, 1, LANES), lambda bb, c: (0, 0, 0)),
            vec(LANES), vec(LANES), vec(inner), vec(inner),
            pl.BlockSpec((SSM_GROUPS, LANES, hpg * L), lambda bb, c: (0, 0, 0)),
            pl.BlockSpec((SSM_GROUPS, LANES, hpg * HEAD_DIM), lambda bb, c: (0, 0, 0)),
        ],
        out_specs=pl.BlockSpec((None, rows, inner), lambda bb, c: (bb, c, 0)),
        out_shape=jax.ShapeDtypeStruct((b, s, inner), BF16),
        scratch_shapes=[
            pltpu.VMEM((n_blk, L + CONV_PAD, LANES), F32),
            pltpu.VMEM((n_blk, L, LANES), F32),
            pltpu.VMEM((SSM_PAIRS, SSM_STATE, LANES), F32),
            pltpu.VMEM((SSM_PAIRS, L, LANES), F32),
            pltpu.VMEM((LANES, L), F32),
            pltpu.VMEM((LANES, L), F32),
            pltpu.VMEM((SSM_PAIRS, 2 * L, LANES), BF16),
            pltpu.VMEM((SSM_PAIRS, L, 2 * L), BF16),
            pltpu.VMEM((SSM_PAIRS, SSM_STATE, 2 * L), BF16),
        ],
        compiler_params=pltpu.CompilerParams(
            dimension_semantics=("parallel", "arbitrary"), vmem_limit_bytes=VMEM_LIMIT),
        name="ssd",
    )(proj3, proj3, dt3, conv_w, conv_b, dt_bias, a_log, d_skip_x, norm_g,
      _head_expansion(L), _head_expansion(HEAD_DIM))


def _mix_kernel(att_ref, yn_ref, ga_ref, gb_ref, x_ref, woa_ref, wos_ref, wout_ref, o_ref):
    ya = jnp.dot(att_ref[...], woa_ref[...], preferred_element_type=F32)
    yb = jnp.dot(yn_ref[...], wos_ref[...], preferred_element_type=F32)
    merged = _sigmoid(ga_ref[...].astype(F32)) * ya + _sigmoid(gb_ref[...].astype(F32)) * yb
    o_ref[...] = x_ref[...] + jnp.dot(merged.astype(BF16), wout_ref[...], preferred_element_type=F32)


def _mix_out(att2d, yn2d, proj2d, x2d, w_oa, w_os, w_out, tm=512):
    m, d = x2d.shape
    ga_col = (proj2d.shape[1] - 2 * d) // d
    row = lambda w, col=0: pl.BlockSpec((tm, w), lambda i, col=col: (i, col))
    full = lambda a: pl.BlockSpec(a.shape, lambda i: (0, 0))
    return pl.pallas_call(
        _mix_kernel,
        grid=(m // tm,),
        in_specs=[row(att2d.shape[1]), row(yn2d.shape[1]), row(d, ga_col), row(d, ga_col + 1), row(d),
                  full(w_oa), full(w_os), full(w_out)],
        out_specs=row(d),
        out_shape=jax.ShapeDtypeStruct((m, d), F32),
        compiler_params=pltpu.CompilerParams(
            dimension_semantics=("parallel",), vmem_limit_bytes=VMEM_LIMIT),
        name="mix_out",
    )(att2d, yn2d, proj2d, proj2d, x2d, w_oa, w_os, w_out)


def _rms(x, g):
    return x * lax.rsqrt(jnp.mean(x * x, axis=-1, keepdims=True) + EPS) * g


def _ffn_kernel(x_ref, g2_ref, wgu_ref, wd_ref, p_ref, g3_ref, wpg_ref, wpp_ref, o_ref):
    x = x_ref[...]
    d_ff = wd_ref.shape[0]
    h = _rms(x, g2_ref[...]).astype(BF16)
    gate = jnp.dot(h, wgu_ref[:, :d_ff], preferred_element_type=F32)
    up = jnp.dot(h, wgu_ref[:, d_ff:], preferred_element_type=F32)
    act = (gate * _sigmoid(gate) * up).astype(BF16)
    x2 = x + jnp.dot(act, wd_ref[...], preferred_element_type=F32)
    h3 = _rms(x2, g3_ref[...]).astype(BF16)
    pgate = _sigmoid(jnp.dot(h3, wpg_ref[...], preferred_element_type=F32))
    proj = jnp.dot(p_ref[...].astype(BF16), wpp_ref[...], preferred_element_type=F32)
    o_ref[...] = x2 + proj * pgate


def _ffn_ple(x2d, g2, w_gu, w_down, p2d, g3, w_pg, w_pp, tm=512):
    m, d = x2d.shape
    vec = lambda a: pl.BlockSpec(a.shape, lambda i: (0, 0))
    resident = lambda a: pl.BlockSpec(a.shape, lambda i: (0, 0), pipeline_mode=pl.Buffered(1))
    return pl.pallas_call(
        _ffn_kernel,
        grid=(m // tm,),
        in_specs=[
            pl.BlockSpec((tm, d), lambda i: (i, 0)),
            vec(g2), resident(w_gu), resident(w_down),
            pl.BlockSpec((tm, p2d.shape[1]), lambda i: (i, 0)),
            vec(g3), resident(w_pg), resident(w_pp),
        ],
        out_specs=pl.BlockSpec((tm, d), lambda i: (i, 0)),
        out_shape=jax.ShapeDtypeStruct((m, d), F32),
        compiler_params=pltpu.CompilerParams(
            dimension_semantics=("parallel",), vmem_limit_bytes=VMEM_LIMIT),
        name="ffn_ple",
    )(x2d, g2, w_gu, w_down, p2d, g3, w_pg, w_pp)


def _head_lanes(v):
    return jnp.pad(jnp.tile(v.astype(F32), 3), (0, LANES - 3 * SSM_HEADS)).reshape(1, LANES)


def _lane_blocks(a):
    r, w = a.shape
    return a.astype(F32).reshape(r, w // LANES, LANES).transpose(1, 0, 2)


def kernel(x, p, ln1_g, w_in, q_norm_g, k_norm_g, w_o_attn, conv_w, conv_b, dt_bias, a_log, d_skip,
           ssm_norm_g, w_o_ssm, w_out, ln2_g, w_gate_up, w_down, ln3_g, w_ple_gate, w_ple_proj):
    b, s, d = x.shape
    assert s % MOBA_BLOCK == 0 and s // MOBA_BLOCK <= MAX_BLOCKS
    attn_w = ATTN_HEADS * HEAD_DIM
    inner = SSM_HEADS * HEAD_DIM
    conv_dim = inner + 2 * SSM_GROUPS * SSM_STATE
    off_z = 3 * attn_w
    off_xbc = off_z + inner
    off_dt = off_xbc + conv_dim
    off_g = off_dt + SSM_HEADS
    row = lambda v: v.astype(F32).reshape(1, -1)

    for i in range(w_in.shape[0]):
        wi = w_in[i]
        tn = 1024
        row_starts = [r for lo, hi in ((0, off_z), (off_xbc, off_dt), (off_z, off_xbc), (off_g, wi.shape[1]))
                      for r in range(lo, hi, tn)]
        w_t = wi.T.astype(BF16)
        w_dt = jnp.pad(jnp.tile(wi[:, off_dt:off_g], (1, 3)), ((0, 0), (0, LANES - 3 * SSM_HEADS))).astype(BF16)

        x2d = x.reshape(b * s, d)
        proj, dt_raw = _in_proj(x2d, row(ln1_g[i]), w_t, row_starts, w_dt, tn=tn)
        proj3 = proj.reshape(b, s, -1)

        qn, kn, sel, vt5 = _attn_prep(
            proj3, row(jnp.tile(q_norm_g[i], ATTN_HEADS)), row(jnp.tile(k_norm_g[i], ATTN_HEADS)))
        score_bound = 1.02 * LOG2E * HEAD_DIM ** 0.5 * jnp.max(jnp.abs(q_norm_g[i])) * jnp.max(jnp.abs(k_norm_g[i]))
        bounded = (score_bound <= SAFE_EXP2_RANGE).astype(jnp.int32).reshape(1)
        att = _moba_attention(bounded, qn, kn, sel, vt5)

        yn = _ssd(proj3, dt_raw.reshape(b, s, LANES), _lane_blocks(conv_w[i]), _lane_blocks(row(conv_b[i])),
                  _head_lanes(dt_bias[i]), _head_lanes(a_log[i]), row(jnp.repeat(d_skip[i], HEAD_DIM)),
                  row(ssm_norm_g[i]))

        x1 = _mix_out(att.reshape(b * s, attn_w), yn.reshape(b * s, inner), proj, x2d,
                      w_o_attn[i].astype(BF16), w_o_ssm[i].astype(BF16), w_out[i].astype(BF16))
        x2 = _ffn_ple(x1, row(ln2_g[i]), w_gate_up[i].astype(BF16), w_down[i].astype(BF16),
                      p[i].reshape(b * s, -1), row(ln3_g[i]), w_ple_gate[i].astype(BF16),
                      w_ple_proj[i].astype(BF16))
        x = x2.reshape(b, s, d)
    return x
```

```python
import math

import jax
import jax.numpy as jnp
from jax import lax
from jax.experimental import pallas as pl
from jax.experimental.pallas import tpu as pltpu

F32 = jnp.float32
BF16 = jnp.bfloat16

EPS = 1e-6
LOG2E = math.log2(math.e)
LANES = 128
HEAD_DIM = 64
ATTN_HEADS = 16
HEAD_PAIRS = ATTN_HEADS // 2
ATTN_PAIRS_PER_STEP = 8
MOBA_BLOCK = 256
MOBA_TOPK = 3
MAX_BLOCKS = 32
SSM_HEADS = 32
SSM_PAIRS = SSM_HEADS // 2
SSM_GROUPS = 4
SSM_STATE = 128
SSM_CONV = 4
SSM_CHUNK = 128
SSM_CHUNKS_PER_STEP = 2
CONV_PAD = 8
NEG = -1e30
SAFE_EXP2_RANGE = 60.0
VMEM_LIMIT = 56 * 1024 * 1024


def _sigmoid(v):
    return 1.0 / (1.0 + jnp.exp(-v))


def _nt_dot(a, b, **kw):
    return lax.dot_general(a, b, (((1,), (1,)), ((), ())), preferred_element_type=F32, **kw)


def _inproj_kernel(x_ref, g_ref, w_ref, wdt_ref, o_ref, dt_ref, h_ref):
    @pl.when(pl.program_id(1) == 0)
    def _():
        x = x_ref[...]
        ms = jnp.mean(x * x, axis=-1, keepdims=True)
        h = (x * lax.rsqrt(ms + EPS) * g_ref[...]).astype(BF16)
        h_ref[...] = h
        dt_ref[...] = jnp.dot(h, wdt_ref[...], preferred_element_type=F32)

    o_ref[...] = _nt_dot(h_ref[...], w_ref[...].astype(BF16)).astype(o_ref.dtype)


def _in_proj(x2d, g, w_t, row_starts, w_dt, tm=2048, tn=1024):
    m, d = x2d.shape
    n = len(row_starts) * tn

    def weight_rows(i, j):
        start = row_starts[-1]
        for k, r in enumerate(row_starts[:-1]):
            start = jnp.where(j == k, r, start)
        return pl.multiple_of(start, math.gcd(*row_starts[1:])), 0

    return pl.pallas_call(
        _inproj_kernel,
        grid=(m // tm, n // tn),
        in_specs=[
            pl.BlockSpec((tm, d), lambda i, j: (i, 0)),
            pl.BlockSpec((1, d), lambda i, j: (0, 0)),
            pl.BlockSpec((pl.Element(tn), pl.Element(d)), weight_rows),
            pl.BlockSpec((d, LANES), lambda i, j: (0, 0)),
        ],
        out_specs=[
            pl.BlockSpec((tm, tn), lambda i, j: (i, j)),
            pl.BlockSpec((tm, LANES), lambda i, j: (i, 0)),
        ],
        out_shape=[jax.ShapeDtypeStruct((m, n), BF16), jax.ShapeDtypeStruct((m, LANES), F32)],
        scratch_shapes=[pltpu.VMEM((tm, d), BF16)],
        compiler_params=pltpu.CompilerParams(
            dimension_semantics=("parallel", "arbitrary"), vmem_limit_bytes=VMEM_LIMIT),
        name="in_proj",
    )(x2d, g, w_t, w_dt)


def _head_rms_norm(x, gain, is_a):
    outs = []
    for c in range(x.shape[1] // LANES):
        blk = x[:, c * LANES:(c + 1) * LANES]
        sq = blk * blk
        sa = jnp.sum(jnp.where(is_a, sq, 0.0), axis=-1, keepdims=True)
        sb = jnp.sum(jnp.where(is_a, 0.0, sq), axis=-1, keepdims=True)
        ms = jnp.where(is_a, sa, sb) * (1.0 / HEAD_DIM)
        outs.append(blk * lax.rsqrt(ms + EPS))
    return jnp.concatenate(outs, axis=1) * gain


def _prep_kernel(q_ref, k_ref, v_ref, qg_ref, kg_ref, qn_ref, kn_ref, sel_ref, vt_ref, kmean_ref):
    i = pl.program_id(1)

    @pl.when(i == 0)
    def _():
        kmean_ref[...] = jnp.zeros_like(kmean_ref)

    lane = lax.broadcasted_iota(jnp.int32, (1, LANES), 1)
    is_a = lane < HEAD_DIM
    qn = _head_rms_norm(q_ref[...].astype(F32), qg_ref[...], is_a) * (HEAD_DIM ** -0.5 * LOG2E)
    kn = _head_rms_norm(k_ref[...].astype(F32), kg_ref[...], is_a)
    kn_ref[...] = kn.astype(BF16)

    vt = v_ref[...].astype(F32).T
    vt_ref[...] = vt.reshape(vt_ref.shape).astype(BF16)

    kmean = kmean_ref[...]
    q_hi = qn.astype(BF16)
    qn_ref[...] = q_hi
    q_lo = (qn - q_hi.astype(F32)).astype(BF16)
    km_hi = kmean.astype(BF16)
    km_lo = (kmean - km_hi.astype(F32)).astype(BF16)
    zero = jnp.zeros((), BF16)
    jidx = lax.broadcasted_iota(jnp.int32, (MAX_BLOCKS, MOBA_BLOCK), 0).astype(F32)
    valid = jidx < i.astype(F32)
    for p in range(HEAD_PAIRS):
        cols = slice(p * LANES, (p + 1) * LANES)
        km4 = jnp.concatenate([jnp.where(is_a, km_hi[:, cols], zero), jnp.where(is_a, zero, km_hi[:, cols]),
                               jnp.where(is_a, km_lo[:, cols], zero), jnp.where(is_a, zero, km_lo[:, cols])],
                              axis=0)
        g_hi = _nt_dot(km4, q_hi[:, cols])
        g_lo = _nt_dot(km4[:2 * MAX_BLOCKS], q_lo[:, cols])
        for hh in range(2):
            r0 = hh * MAX_BLOCKS
            gate = (g_hi[r0:r0 + MAX_BLOCKS] + g_hi[r0 + 2 * MAX_BLOCKS:r0 + 3 * MAX_BLOCKS]
                    + g_lo[r0:r0 + MAX_BLOCKS])
            gate = jnp.where(valid, gate, -jnp.inf)
            sel = jnp.zeros_like(gate)
            for _ in range(MOBA_TOPK):
                top = jnp.max(gate, axis=0, keepdims=True)
                first = jnp.min(jnp.where(gate == top, jidx, float(MAX_BLOCKS)), axis=0, keepdims=True)
                pick = jidx == first
                sel = jnp.where(pick, 1.0, sel)
                gate = jnp.where(pick, -jnp.inf, gate)
            sel_ref[2 * p + hh] = jnp.where(valid, sel, 0.0)

    kmean_ref[pl.ds(i, 1), :] = jnp.mean(kn, axis=0, keepdims=True)


def _attn_prep(proj3, qg, kg):
    b, s, _ = proj3.shape
    nb = s // MOBA_BLOCK
    width = ATTN_HEADS * HEAD_DIM
    tile = lambda col: pl.BlockSpec((None, MOBA_BLOCK, width), lambda bb, i, col=col: (bb, i, col))
    gain = pl.BlockSpec((1, width), lambda bb, i: (0, 0))
    return pl.pallas_call(
        _prep_kernel,
        grid=(b, nb),
        in_specs=[tile(0), tile(1), tile(2), gain, gain],
        out_specs=[
            tile(0), tile(0),
            pl.BlockSpec((None, ATTN_HEADS, None, MAX_BLOCKS, MOBA_BLOCK), lambda bb, i: (bb, 0, i, 0, 0)),
            pl.BlockSpec((None, HEAD_PAIRS, None, LANES, MOBA_BLOCK), lambda bb, i: (bb, 0, i, 0, 0)),
        ],
        out_shape=[
            jax.ShapeDtypeStruct((b, s, width), BF16),
            jax.ShapeDtypeStruct((b, s, width), BF16),
            jax.ShapeDtypeStruct((b, ATTN_HEADS, nb, MAX_BLOCKS, MOBA_BLOCK), F32),
            jax.ShapeDtypeStruct((b, HEAD_PAIRS, nb, LANES, MOBA_BLOCK), BF16),
        ],
        scratch_shapes=[pltpu.VMEM((MAX_BLOCKS, width), F32)],
        compiler_params=pltpu.CompilerParams(
            dimension_semantics=("parallel", "arbitrary"), vmem_limit_bytes=VMEM_LIMIT),
        name="attn_prep",
    )(proj3, proj3, proj3, qg, kg)


def _attn_kernel(bounded_ref, q_ref, k_ref, vt_ref, sel_ref, o_ref, qm_ref, acc_ref, s_ref, p_ref):
    i = pl.program_id(2)
    heads = 2 * ATTN_PAIRS_PER_STEP
    lane = lax.broadcasted_iota(jnp.int32, (1, LANES), 1)
    is_a = lane < HEAD_DIM
    zero = jnp.zeros((), BF16)
    for pp in range(ATTN_PAIRS_PER_STEP):
        qp = q_ref[:, pp * LANES:(pp + 1) * LANES]
        qm_ref[2 * pp] = jnp.where(is_a, qp, zero)
        qm_ref[2 * pp + 1] = jnp.where(is_a, zero, qp)

    def scores(j, h):
        pp = h // 2
        kt = k_ref[pl.ds(pl.multiple_of(j * MOBA_BLOCK, MOBA_BLOCK), MOBA_BLOCK), pp * LANES:(pp + 1) * LANES]
        return _nt_dot(kt, qm_ref[h])

    def values(j, h):
        return vt_ref[h // 2, j, (h % 2) * HEAD_DIM:(h % 2 + 1) * HEAD_DIM, :]

    krow = lax.broadcasted_iota(jnp.int32, (MOBA_BLOCK, MOBA_BLOCK), 0)
    qcol = lax.broadcasted_iota(jnp.int32, (MOBA_BLOCK, MOBA_BLOCK), 1)
    causal = krow <= qcol
    zero_row = jnp.zeros((1, MOBA_BLOCK), F32)

    def bounded_scores():
        ls = [zero_row] * heads
        for h in range(heads):
            p = jnp.exp2(jnp.where(causal, scores(i, h), NEG))
            ls[h] = jnp.sum(p, axis=0, keepdims=True)
            p_ref[0, h] = p.astype(BF16)
        for h in range(heads):
            acc_ref[h] = jnp.dot(values(i, h), p_ref[0, h], preferred_element_type=F32)

        def body(jj, ls):
            ls = list(ls)
            for u in range(2):
                j = 2 * jj + u
                for h in range(heads):
                    picked = sel_ref[h, pl.ds(j, 1), :] > 0.0
                    if u == 1:
                        picked = jnp.logical_and(picked, j < i)
                    p = jnp.exp2(scores(j, h) + jnp.where(picked, 0.0, NEG))
                    ls[h] = ls[h] + jnp.sum(p, axis=0, keepdims=True)
                    p_ref[u, h] = p.astype(BF16)
            for u in range(2):
                for h in range(heads):
                    acc_ref[h] += jnp.dot(values(2 * jj + u, h), p_ref[u, h], preferred_element_type=F32)
            return tuple(ls)

        return lax.fori_loop(0, (i + 1) // 2, body, tuple(ls))

    def running_max():
        m0, l0 = [], []
        for h in range(heads):
            s = jnp.where(causal, scores(i, h), NEG)
            m0.append(jnp.max(s, axis=0, keepdims=True))
            s_ref[h] = s
        for h in range(heads):
            p = jnp.exp2(s_ref[h] - m0[h])
            l0.append(jnp.sum(p, axis=0, keepdims=True))
            p_ref[0, h] = p.astype(BF16)
        for h in range(heads):
            acc_ref[h] = jnp.dot(values(i, h), p_ref[0, h], preferred_element_type=F32)

        def body(j, carry):
            ms, ls = carry
            tile_max, new_m, new_l, alphas = [], [], [], []
            for h in range(heads):
                s = scores(j, h)
                tile_max.append(jnp.max(s, axis=0, keepdims=True))
                s_ref[h] = s
            for h in range(heads):
                picked = sel_ref[h, pl.ds(j, 1), :] > 0.0
                m = jnp.maximum(ms[h], jnp.where(picked, tile_max[h], NEG))
                p = jnp.exp2(s_ref[h] - jnp.where(picked, m, -NEG))
                alpha = jnp.exp2(ms[h] - m)
                new_m.append(m)
                new_l.append(alpha * ls[h] + jnp.sum(p, axis=0, keepdims=True))
                alphas.append(alpha)
                p_ref[0, h] = p.astype(BF16)
            for h in range(heads):
                acc_ref[h] = alphas[h] * acc_ref[h] + jnp.dot(values(j, h), p_ref[0, h],
                                                              preferred_element_type=F32)
            return tuple(new_m), tuple(new_l)

        return lax.fori_loop(0, i, body, (tuple(m0), tuple(l0)))[1]

    ls = lax.cond(bounded_ref[0] != 0, bounded_scores, running_max)
    for pp in range(ATTN_PAIRS_PER_STEP):
        out_t = jnp.concatenate([acc_ref[2 * pp] * (1.0 / ls[2 * pp]),
                                 acc_ref[2 * pp + 1] * (1.0 / ls[2 * pp + 1])], axis=0)
        o_ref[:, pp * LANES:(pp + 1) * LANES] = out_t.T.astype(o_ref.dtype)


def _moba_attention(bounded, qn, kn, sel, vt5):
    b, s, width = qn.shape
    nb = s // MOBA_BLOCK
    pps = ATTN_PAIRS_PER_STEP
    gw = pps * LANES
    qtile = pl.BlockSpec((None, MOBA_BLOCK, gw), lambda bb, g, i: (bb, i, g))
    return pl.pallas_call(
        _attn_kernel,
        grid=(b, HEAD_PAIRS // pps, nb),
        in_specs=[
            pl.BlockSpec(memory_space=pltpu.SMEM),
            qtile,
            pl.BlockSpec((None, s, gw), lambda bb, g, i: (bb, 0, g), pipeline_mode=pl.Buffered(1)),
            pl.BlockSpec((None, pps, nb, LANES, MOBA_BLOCK), lambda bb, g, i: (bb, g, 0, 0, 0),
                         pipeline_mode=pl.Buffered(1)),
            pl.BlockSpec((None, 2 * pps, None, MAX_BLOCKS, MOBA_BLOCK), lambda bb, g, i: (bb, g, i, 0, 0)),
        ],
        out_specs=qtile,
        out_shape=jax.ShapeDtypeStruct((b, s, width), BF16),
        scratch_shapes=[
            pltpu.VMEM((2 * pps, MOBA_BLOCK, LANES), BF16),
            pltpu.VMEM((2 * pps, HEAD_DIM, MOBA_BLOCK), F32),
            pltpu.VMEM((2 * pps, MOBA_BLOCK, MOBA_BLOCK), F32),
            pltpu.VMEM((2, 2 * pps, MOBA_BLOCK, MOBA_BLOCK), BF16),
        ],
        compiler_params=pltpu.CompilerParams(
            dimension_semantics=("parallel", "parallel", "arbitrary"), vmem_limit_bytes=VMEM_LIMIT),
        name="moba_attn",
    )(bounded, qn, kn, vt5, sel)


def _ssd_kernel(xbc_ref, z_ref, dt_ref, cw_ref, cb_ref, dtb_ref, alog_ref, dsk_ref, ng_ref, ec_ref, ew_ref,
                o_ref, ext_ref, act_ref, state_ref, *chunk_scratch):
    @pl.when(pl.program_id(1) == 0)
    def _():
        ext_ref[:, 0:CONV_PAD, :] = jnp.zeros((ext_ref.shape[0], CONV_PAD, LANES), F32)
        state_ref[...] = jnp.zeros_like(state_ref)

    for sub in range(SSM_CHUNKS_PER_STEP):
        rows = pl.ds(sub * SSM_CHUNK, SSM_CHUNK)
        _ssd_chunk(xbc_ref.at[rows], z_ref.at[rows], dt_ref.at[rows], cw_ref, cb_ref, dtb_ref, alog_ref, dsk_ref,
                   ng_ref, ec_ref, ew_ref, o_ref.at[rows], ext_ref, act_ref, state_ref, *chunk_scratch)


def _ssd_chunk(xbc_ref, z_ref, dt_ref, cw_ref, cb_ref, dtb_ref, alog_ref, dsk_ref, ng_ref, ec_ref, ew_ref,
               o_ref, ext_ref, act_ref, state_ref, y_ref, acst_ref, ldtt_ref, rhs_ref, ly_ref, ls_ref):
    L = SSM_CHUNK
    b_blk = SSM_PAIRS
    c_blk = SSM_PAIRS + SSM_GROUPS

    def conv_silu(blk):
        ext_ref[blk, CONV_PAD:CONV_PAD + L, :] = xbc_ref[:, blk * LANES:(blk + 1) * LANES].astype(F32)
        w = cw_ref[blk]
        conv = cb_ref[blk]
        for k in range(SSM_CONV):
            start = CONV_PAD - (SSM_CONV - 1) + k
            conv = conv + w[k:k + 1, :] * ext_ref[blk, start:start + L, :]
        ext_ref[blk, 0:CONV_PAD, :] = ext_ref[blk, L:L + CONV_PAD, :]
        act_ref[blk] = conv * _sigmoid(conv)

    dt_in = dt_ref[...] + dtb_ref[...]
    dtv = jnp.maximum(dt_in, 0.0) + jnp.log(1.0 + jnp.exp(-jnp.abs(dt_in)))
    a2 = dtv * (-LOG2E * jnp.exp(alog_ref[...]))
    row = lax.broadcasted_iota(jnp.int32, (L, L), 0)
    col = lax.broadcasted_iota(jnp.int32, (L, L), 1)
    tri = col <= row
    acs = jnp.dot(tri.astype(F32), a2, preferred_element_type=F32, precision=lax.Precision.HIGHEST)
    acst_ref[...] = acs.T
    ldtt_ref[...] = (jnp.log(dtv) * LOG2E).T
    lane = lax.broadcasted_iota(jnp.int32, (1, LANES), 1)
    is_a = lane < HEAD_DIM
    zero = jnp.zeros((), BF16)

    def split3(v):
        hi = v.astype(BF16)
        r1 = v - hi.astype(F32)
        mid = r1.astype(BF16)
        lo = (r1 - mid.astype(F32)).astype(BF16)
        return jnp.where(lane < SSM_HEADS, hi, jnp.where(lane < 2 * SSM_HEADS, mid, lo))

    acs_parts = split3(acs)
    wst_parts = split3(jnp.exp2(acs))
    ppg = SSM_PAIRS // SSM_GROUPS

    for g in range(SSM_GROUPS):
        for blk in list(range(g * ppg, (g + 1) * ppg)) + [b_blk + g, c_blk + g]:
            conv_silu(blk)
        bg = act_ref[b_blk + g]
        cbm = _nt_dot(act_ref[c_blk + g].astype(BF16), bg.astype(BF16))
        bt = bg.T
        cmat = jnp.dot(acs_parts, ec_ref[g], preferred_element_type=F32)
        wst = jnp.dot(wst_parts, ew_ref[g], preferred_element_type=F32)
        prev = jnp.concatenate([state_ref[g * ppg + pj] for pj in range(ppg)], axis=1).astype(BF16)
        y_off = jnp.dot(act_ref[c_blk + g].astype(BF16), prev, preferred_element_type=F32) * wst
        decs = []
        for pj in range(ppg):
            xp = act_ref[g * ppg + pj].astype(BF16)
            rhs_ref[g * ppg + pj] = jnp.concatenate([jnp.where(is_a, xp, zero), jnp.where(is_a, zero, xp)], axis=0)
            lhs_y, lhs_s, dec = [], [], []
            for hh in range(2):
                k = 2 * pj + hh
                h = g * 2 * ppg + k
                arow = acst_ref[h:h + 1, :]
                rrow = arow - ldtt_ref[h:h + 1, :]
                expo = jnp.where(tri, cmat[:, k * LANES:(k + 1) * LANES] - rrow, NEG)
                lhs_y.append((jnp.exp2(expo) * cbm).astype(BF16))
                tot = arow[:, L - 1:L]
                lhs_s.append((bt * jnp.exp2(tot - rrow)).astype(BF16))
                dec.append(jnp.exp2(tot))
            ly_ref[g * ppg + pj] = jnp.concatenate(lhs_y, axis=1)
            ls_ref[g * ppg + pj] = jnp.concatenate(lhs_s, axis=1)
            decs.append(jnp.where(is_a, dec[0], dec[1]))
        for pj in range(ppg):
            y_ref[g * ppg + pj] = (jnp.dot(ly_ref[g * ppg + pj], rhs_ref[g * ppg + pj], preferred_element_type=F32)
                                   + y_off[:, pj * LANES:(pj + 1) * LANES])
        for pj in range(ppg):
            j = g * ppg + pj
            state_ref[j] = state_ref[j] * decs[pj] + jnp.dot(ls_ref[g * ppg + pj], rhs_ref[g * ppg + pj],
                                                            preferred_element_type=F32)

        ys = []
        ssq = jnp.zeros((L, 1), F32)
        for j in range(g * ppg, (g + 1) * ppg):
            cols = slice(j * LANES, (j + 1) * LANES)
            zf = z_ref[:, cols].astype(F32)
            y = (y_ref[j] + dsk_ref[:, cols] * act_ref[j]) * (zf * _sigmoid(zf))
            ssq = ssq + jnp.sum(y * y, axis=-1, keepdims=True)
            ys.append(y)
        scale = lax.rsqrt(ssq * (1.0 / (ppg * LANES)) + EPS)
        for j, y in zip(range(g * ppg, (g + 1) * ppg), ys):
            cols = slice(j * LANES, (j + 1) * LANES)
            o_ref[:, cols] = (y * scale * ng_ref[:, cols]).astype(o_ref.dtype)


def _head_expansion(width):
    r = lax.broadcasted_iota(jnp.int32, (LANES, SSM_HEADS * width), 0)
    col = lax.broadcasted_iota(jnp.int32, (LANES, SSM_HEADS * width), 1)
    e = jnp.logical_and(r < 3 * SSM_HEADS, r % SSM_HEADS == col // width).astype(BF16)
    return e.reshape(LANES, SSM_GROUPS, -1).transpose(1, 0, 2)


def _ssd(proj3, dt3, conv_w, conv_b, dt_bias, a_log, d_skip_x, norm_g):
    b, s, _ = proj3.shape
    L = SSM_CHUNK
    inner = SSM_HEADS * HEAD_DIM
    conv_dim = inner + 2 * SSM_GROUPS * SSM_STATE
    n_blk = conv_dim // LANES
    hpg = SSM_HEADS // SSM_GROUPS
    ppg = SSM_PAIRS // SSM_GROUPS
    rows = SSM_CHUNKS_PER_STEP * L
    vec = lambda w: pl.BlockSpec((1, w), lambda bb, c: (0, 0))
    return pl.pallas_call(
        _ssd_kernel,
        grid=(b, s // rows),
        in_specs=[
            pl.BlockSpec((None, rows, conv_dim), lambda bb, c: (bb, c, 1)),
            pl.BlockSpec((None, rows, inner), lambda bb, c: (bb, c, 3)),
            pl.BlockSpec((None, rows, LANES), lambda bb, c: (bb, c, 0)),
            pl.BlockSpec((n_blk, SSM_CONV, LANES), lambda bb, c: (0, 0, 0)),
            pl.BlockSpec((n_blk, 1, LANES), lambda bb, c: (0, 0, 0)),
            vec(LANES), vec(LANES), vec(inner), vec(inner),
            pl.BlockSpec((SSM_GROUPS, LANES, hpg * L), lambda bb, c: (0, 0, 0)),
            pl.BlockSpec((SSM_GROUPS, LANES, hpg * HEAD_DIM), lambda bb, c: (0, 0, 0)),
        ],
        out_specs=pl.BlockSpec((None, rows, inner), lambda bb, c: (bb, c, 0)),
        out_shape=jax.ShapeDtypeStruct((b, s, inner), BF16),
        scratch_shapes=[
            pltpu.VMEM((n_blk, L + CONV_PAD, LANES), F32),
            pltpu.VMEM((n_blk, L, LANES), F32),
            pltpu.VMEM((SSM_PAIRS, SSM_STATE, LANES), F32),
            pltpu.VMEM((SSM_PAIRS, L, LANES), F32),
            pltpu.VMEM((LANES, L), F32),
            pltpu.VMEM((LANES, L), F32),
            pltpu.VMEM((SSM_PAIRS, 2 * L, LANES), BF16),
            pltpu.VMEM((SSM_PAIRS, L, 2 * L), BF16),
            pltpu.VMEM((SSM_PAIRS, SSM_STATE, 2 * L), BF16),
        ],
        compiler_params=pltpu.CompilerParams(
            dimension_semantics=("parallel", "arbitrary"), vmem_limit_bytes=VMEM_LIMIT),
        name="ssd",
    )(proj3, proj3, dt3, conv_w, conv_b, dt_bias, a_log, d_skip_x, norm_g,
      _head_expansion(L), _head_expansion(HEAD_DIM))


def _mix_kernel(att_ref, yn_ref, ga_ref, gb_ref, x_ref, woa_ref, wos_ref, wout_ref, o_ref):
    ya = jnp.dot(att_ref[...], woa_ref[...], preferred_element_type=F32)
    yb = jnp.dot(yn_ref[...], wos_ref[...], preferred_element_type=F32)
    merged = _sigmoid(ga_ref[...].astype(F32)) * ya + _sigmoid(gb_ref[...].astype(F32)) * yb
    o_ref[...] = x_ref[...] + jnp.dot(merged.astype(BF16), wout_ref[...], preferred_element_type=F32)


def _mix_out(att2d, yn2d, proj2d, x2d, w_oa, w_os, w_out, tm=512):
    m, d = x2d.shape
    ga_col = (proj2d.shape[1] - 2 * d) // d
    row = lambda w, col=0: pl.BlockSpec((tm, w), lambda i, col=col: (i, col))
    full = lambda a: pl.BlockSpec(a.shape, lambda i: (0, 0))
    return pl.pallas_call(
        _mix_kernel,
        grid=(m // tm,),
        in_specs=[row(att2d.shape[1]), row(yn2d.shape[1]), row(d, ga_col), row(d, ga_col + 1), row(d),
                  full(w_oa), full(w_os), full(w_out)],
        out_specs=row(d),
        out_shape=jax.ShapeDtypeStruct((m, d), F32),
        compiler_params=pltpu.CompilerParams(
            dimension_semantics=("parallel",), vmem_limit_bytes=VMEM_LIMIT),
        name="mix_out",
    )(att2d, yn2d, proj2d, proj2d, x2d, w_oa, w_os, w_out)


def _rms(x, g):
    return x * lax.rsqrt(jnp.mean(x * x, axis=-1, keepdims=True) + EPS) * g


def _ffn_kernel(x_ref, g2_ref, wgu_ref, wd_ref, p_ref, g3_ref, wpg_ref, wpp_ref, o_ref):
    x = x_ref[...]
    d_ff = wd_ref.shape[0]
    h = _rms(x, g2_ref[...]).astype(BF16)
    gate = jnp.dot(h, wgu_ref[:, :d_ff], preferred_element_type=F32)
    up = jnp.dot(h, wgu_ref[:, d_ff:], preferred_element_type=F32)
    act = (gate * _sigmoid(gate) * up).astype(BF16)
    x2 = x + jnp.dot(act, wd_ref[...], preferred_element_type=F32)
    h3 = _rms(x2, g3_ref[...]).astype(BF16)
    pgate = _sigmoid(jnp.dot(h3, wpg_ref[...], preferred_element_type=F32))
    proj = jnp.dot(p_ref[...].astype(BF16), wpp_ref[...], preferred_element_type=F32)
    o_ref[...] = x2 + proj * pgate


def _ffn_ple(x2d, g2, w_gu, w_down, p2d, g3, w_pg, w_pp, tm=512):
    m, d = x2d.shape
    vec = lambda a: pl.BlockSpec(a.shape, lambda i: (0, 0))
    resident = lambda a: pl.BlockSpec(a.shape, lambda i: (0, 0), pipeline_mode=pl.Buffered(1))
    return pl.pallas_call(
        _ffn_kernel,
        grid=(m // tm,),
        in_specs=[
            pl.BlockSpec((tm, d), lambda i: (i, 0)),
            vec(g2), resident(w_gu), resident(w_down),
            pl.BlockSpec((tm, p2d.shape[1]), lambda i: (i, 0)),
            vec(g3), resident(w_pg), resident(w_pp),
        ],
        out_specs=pl.BlockSpec((tm, d), lambda i: (i, 0)),
        out_shape=jax.ShapeDtypeStruct((m, d), F32),
        compiler_params=pltpu.CompilerParams(
            dimension_semantics=("parallel",), vmem_limit_bytes=VMEM_LIMIT),
        name="ffn_ple",
    )(x2d, g2, w_gu, w_down, p2d, g3, w_pg, w_pp)


def _head_lanes(v):
    return jnp.pad(jnp.tile(v.astype(F32), 3), (0, LANES - 3 * SSM_HEADS)).reshape(1, LANES)


def _lane_blocks(a):
    r, w = a.shape
    return a.astype(F32).reshape(r, w // LANES, LANES).transpose(1, 0, 2)


def kernel(x, p, ln1_g, w_in, q_norm_g, k_norm_g, w_o_attn, conv_w, conv_b, dt_bias, a_log, d_skip,
           ssm_norm_g, w_o_ssm, w_out, ln2_g, w_gate_up, w_down, ln3_g, w_ple_gate, w_ple_proj):
    b, s, d = x.shape
    assert s % MOBA_BLOCK == 0 and s // MOBA_BLOCK <= MAX_BLOCKS
    attn_w = ATTN_HEADS * HEAD_DIM
    inner = SSM_HEADS * HEAD_DIM
    conv_dim = inner + 2 * SSM_GROUPS * SSM_STATE
    off_z = 3 * attn_w
    off_xbc = off_z + inner
    off_dt = off_xbc + conv_dim
    off_g = off_dt + SSM_HEADS
    row = lambda v: v.astype(F32).reshape(1, -1)

    for i in range(w_in.shape[0]):
        wi = w_in[i]
        tn = 1024
        row_starts = [r for lo, hi in ((0, off_z), (off_xbc, off_dt), (off_z, off_xbc), (off_g, wi.shape[1]))
                      for r in range(lo, hi, tn)]
        w_t = wi.T
        w_dt = jnp.pad(jnp.tile(wi[:, off_dt:off_g], (1, 3)), ((0, 0), (0, LANES - 3 * SSM_HEADS))).astype(BF16)

        x2d = x.reshape(b * s, d)
        proj, dt_raw = _in_proj(x2d, row(ln1_g[i]), w_t, row_starts, w_dt, tn=tn)
        proj3 = proj.reshape(b, s, -1)

        qn, kn, sel, vt5 = _attn_prep(
            proj3, row(jnp.tile(q_norm_g[i], ATTN_HEADS)), row(jnp.tile(k_norm_g[i], ATTN_HEADS)))
        score_bound = 1.02 * LOG2E * HEAD_DIM ** 0.5 * jnp.max(jnp.abs(q_norm_g[i])) * jnp.max(jnp.abs(k_norm_g[i]))
        bounded = (score_bound <= SAFE_EXP2_RANGE).astype(jnp.int32).reshape(1)
        att = _moba_attention(bounded, qn, kn, sel, vt5)

        yn = _ssd(proj3, dt_raw.reshape(b, s, LANES), _lane_blocks(conv_w[i]), _lane_blocks(row(conv_b[i])),
                  _head_lanes(dt_bias[i]), _head_lanes(a_log[i]), row(jnp.repeat(d_skip[i], HEAD_DIM)),
                  row(ssm_norm_g[i]))

        x1 = _mix_out(att.reshape(b * s, attn_w), yn.reshape(b * s, inner), proj, x2d,
                      w_o_attn[i].astype(BF16), w_o_ssm[i].astype(BF16), w_out[i].astype(BF16))
        x2 = _ffn_ple(x1, row(ln2_g[i]), w_gate_up[i].astype(BF16), w_down[i].astype(BF16),
                      p[i].reshape(b * s, -1), row(ln3_g[i]), w_ple_gate[i].astype(BF16),
                      w_ple_proj[i].astype(BF16))
        x = x2.reshape(b, s, d)
    return x
```

```python
import math

import jax
import jax.numpy as jnp
from jax import lax
from jax.experimental import pallas as pl
from jax.experimental.pallas import tpu as pltpu

F32 = jnp.float32
BF16 = jnp.bfloat16

EPS = 1e-6
LOG2E = math.log2(math.e)
LANES = 128
HEAD_DIM = 64
ATTN_HEADS = 16
HEAD_PAIRS = ATTN_HEADS // 2
ATTN_PAIRS_PER_STEP = 8
MOBA_BLOCK = 256
MOBA_TOPK = 3
MAX_BLOCKS = 32
SSM_HEADS = 32
SSM_PAIRS = SSM_HEADS // 2
SSM_GROUPS = 4
SSM_STATE = 128
SSM_CONV = 4
SSM_CHUNK = 128
SSM_CHUNKS_PER_STEP = 2
CONV_PAD = 8
NEG = -1e30
SAFE_EXP2_RANGE = 60.0
VMEM_LIMIT = 56 * 1024 * 1024


def _sigmoid(v):
    return 1.0 / (1.0 + jnp.exp(-v))


def _nt_dot(a, b, **kw):
    return lax.dot_general(a, b, (((1,), (1,)), ((), ())), preferred_element_type=F32, **kw)


def _inproj_kernel(x_ref, g_ref, w_ref, wdt_ref, o_ref, dt_ref, h_ref):
    @pl.when(pl.program_id(1) == 0)
    def _():
        x = x_ref[...]
        ms = jnp.mean(x * x, axis=-1, keepdims=True)
        h = (x * lax.rsqrt(ms + EPS) * g_ref[...]).astype(BF16)
        h_ref[...] = h
        dt_ref[...] = jnp.dot(h, wdt_ref[...], preferred_element_type=F32)

    o_ref[...] = _nt_dot(h_ref[...], w_ref[...].astype(BF16)).astype(o_ref.dtype)


def _in_proj(x2d, g, w_t, row_starts, w_dt, tm=2048, tn=1024):
    m, d = x2d.shape
    n = len(row_starts) * tn

    def weight_rows(i, j):
        start = row_starts[-1]
        for k, r in enumerate(row_starts[:-1]):
            start = jnp.where(j == k, r, start)
        return pl.multiple_of(start, math.gcd(*row_starts[1:])), 0

    return pl.pallas_call(
        _inproj_kernel,
        grid=(m // tm, n // tn),
        in_specs=[
            pl.BlockSpec((tm, d), lambda i, j: (i, 0)),
            pl.BlockSpec((1, d), lambda i, j: (0, 0)),
            pl.BlockSpec((pl.Element(tn), pl.Element(d)), weight_rows),
            pl.BlockSpec((d, LANES), lambda i, j: (0, 0)),
        ],
        out_specs=[
            pl.BlockSpec((tm, tn), lambda i, j: (i, j)),
            pl.BlockSpec((tm, LANES), lambda i, j: (i, 0)),
        ],
        out_shape=[jax.ShapeDtypeStruct((m, n), BF16), jax.ShapeDtypeStruct((m, LANES), F32)],
        scratch_shapes=[pltpu.VMEM((tm, d), BF16)],
        compiler_params=pltpu.CompilerParams(
            dimension_semantics=("parallel", "arbitrary"), vmem_limit_bytes=VMEM_LIMIT),
        name="in_proj",
    )(x2d, g, w_t, w_dt)


def _head_rms_norm(x, gain, is_a):
    outs = []
    for c in range(x.shape[1] // LANES):
        blk = x[:, c * LANES:(c + 1) * LANES]
        sq = blk * blk
        sa = jnp.sum(jnp.where(is_a, sq, 0.0), axis=-1, keepdims=True)
        sb = jnp.sum(jnp.where(is_a, 0.0, sq), axis=-1, keepdims=True)
        ms = jnp.where(is_a, sa, sb) * (1.0 / HEAD_DIM)
        outs.append(blk * lax.rsqrt(ms + EPS))
    return jnp.concatenate(outs, axis=1) * gain


def _prep_kernel(q_ref, k_ref, v_ref, qg_ref, kg_ref, qn_ref, kn_ref, sel_ref, vt_ref, kmean_ref):
    i = pl.program_id(1)

    @pl.when(i == 0)
    def _():
        kmean_ref[...] = jnp.zeros_like(kmean_ref)

    lane = lax.broadcasted_iota(jnp.int32, (1, LANES), 1)
    is_a = lane < HEAD_DIM
    qn = _head_rms_norm(q_ref[...].astype(F32), qg_ref[...], is_a) * (HEAD_DIM ** -0.5 * LOG2E)
    kn = _head_rms_norm(k_ref[...].astype(F32), kg_ref[...], is_a)
    kn_ref[...] = kn.astype(BF16)

    vt = v_ref[...].astype(F32).T
    vt_ref[...] = vt.reshape(vt_ref.shape).astype(BF16)

    kmean = kmean_ref[...]
    q_hi = qn.astype(BF16)
    qn_ref[...] = q_hi
    q_lo = (qn - q_hi.astype(F32)).astype(BF16)
    km_hi = kmean.astype(BF16)
    km_lo = (kmean - km_hi.astype(F32)).astype(BF16)
    zero = jnp.zeros((), BF16)
    jidx = lax.broadcasted_iota(jnp.int32, (MAX_BLOCKS, MOBA_BLOCK), 0).astype(F32)
    valid = jidx < i.astype(F32)
    for p in range(HEAD_PAIRS):
        cols = slice(p * LANES, (p + 1) * LANES)
        km4 = jnp.concatenate([jnp.where(is_a, km_hi[:, cols], zero), jnp.where(is_a, zero, km_hi[:, cols]),
                               jnp.where(is_a, km_lo[:, cols], zero), jnp.where(is_a, zero, km_lo[:, cols])],
                              axis=0)
        g_hi = _nt_dot(km4, q_hi[:, cols])
        g_lo = _nt_dot(km4[:2 * MAX_BLOCKS], q_lo[:, cols])
        for hh in range(2):
            r0 = hh * MAX_BLOCKS
            gate = (g_hi[r0:r0 + MAX_BLOCKS] + g_hi[r0 + 2 * MAX_BLOCKS:r0 + 3 * MAX_BLOCKS]
                    + g_lo[r0:r0 + MAX_BLOCKS])
            gate = jnp.where(valid, gate, -jnp.inf)
            sel = jnp.zeros_like(gate)
            for _ in range(MOBA_TOPK):
                top = jnp.max(gate, axis=0, keepdims=True)
                first = jnp.min(jnp.where(gate == top, jidx, float(MAX_BLOCKS)), axis=0, keepdims=True)
                pick = jidx == first
                sel = jnp.where(pick, 1.0, sel)
                gate = jnp.where(pick, -jnp.inf, gate)
            sel_ref[2 * p + hh] = jnp.where(valid, sel, 0.0)

    kmean_ref[pl.ds(i, 1), :] = jnp.mean(kn, axis=0, keepdims=True)


def _attn_prep(proj3, qg, kg):
    b, s, _ = proj3.shape
    nb = s // MOBA_BLOCK
    width = ATTN_HEADS * HEAD_DIM
    tile = lambda col: pl.BlockSpec((None, MOBA_BLOCK, width), lambda bb, i, col=col: (bb, i, col))
    gain = pl.BlockSpec((1, width), lambda bb, i: (0, 0))
    return pl.pallas_call(
        _prep_kernel,
        grid=(b, nb),
        in_specs=[tile(0), tile(1), tile(2), gain, gain],
        out_specs=[
            tile(0), tile(0),
            pl.BlockSpec((None, ATTN_HEADS, None, MAX_BLOCKS, MOBA_BLOCK), lambda bb, i: (bb, 0, i, 0, 0)),
            pl.BlockSpec((None, HEAD_PAIRS, None, LANES, MOBA_BLOCK), lambda bb, i: (bb, 0, i, 0, 0)),
        ],
        out_shape=[
            jax.ShapeDtypeStruct((b, s, width), BF16),
            jax.ShapeDtypeStruct((b, s, width), BF16),
            jax.ShapeDtypeStruct((b, ATTN_HEADS, nb, MAX_BLOCKS, MOBA_BLOCK), F32),
            jax.ShapeDtypeStruct((b, HEAD_PAIRS, nb, LANES, MOBA_BLOCK), BF16),
        ],
        scratch_shapes=[pltpu.VMEM((MAX_BLOCKS, width), F32)],
        compiler_params=pltpu.CompilerParams(
            dimension_semantics=("parallel", "arbitrary"), vmem_limit_bytes=VMEM_LIMIT),
        name="attn_prep",
    )(proj3, proj3, proj3, qg, kg)


def _attn_kernel(bounded_ref, q_ref, k_ref, vt_ref, sel_ref, o_ref, qm_ref, acc_ref, s_ref, p_ref):
    i = pl.program_id(2)
    heads = 2 * ATTN_PAIRS_PER_STEP
    lane = lax.broadcasted_iota(jnp.int32, (1, LANES), 1)
    is_a = lane < HEAD_DIM
    zero = jnp.zeros((), BF16)
    for pp in range(ATTN_PAIRS_PER_STEP):
        qp = q_ref[:, pp * LANES:(pp + 1) * LANES]
        qm_ref[2 * pp] = jnp.where(is_a, qp, zero)
        qm_ref[2 * pp + 1] = jnp.where(is_a, zero, qp)

    def scores(j, h):
        pp = h // 2
        kt = k_ref[pl.ds(pl.multiple_of(j * MOBA_BLOCK, MOBA_BLOCK), MOBA_BLOCK), pp * LANES:(pp + 1) * LANES]
        return _nt_dot(kt, qm_ref[h])

    def values(j, h):
        return vt_ref[h // 2, j, (h % 2) * HEAD_DIM:(h % 2 + 1) * HEAD_DIM, :]

    krow = lax.broadcasted_iota(jnp.int32, (MOBA_BLOCK, MOBA_BLOCK), 0)
    qcol = lax.broadcasted_iota(jnp.int32, (MOBA_BLOCK, MOBA_BLOCK), 1)
    causal = krow <= qcol
    zero_row = jnp.zeros((1, MOBA_BLOCK), F32)

    def bounded_scores():
        ls = [zero_row] * heads
        for h in range(heads):
            p = jnp.exp2(jnp.where(causal, scores(i, h), NEG))
            ls[h] = jnp.sum(p, axis=0, keepdims=True)
            p_ref[0, h] = p.astype(BF16)
        for h in range(heads):
            acc_ref[h] = jnp.dot(values(i, h), p_ref[0, h], preferred_element_type=F32)

        def body(jj, ls):
            ls = list(ls)
            for u in range(2):
                j = 2 * jj + u
                for h in range(heads):
                    picked = sel_ref[h, pl.ds(j, 1), :] > 0.0
                    if u == 1:
                        picked = jnp.logical_and(picked, j < i)
                    p = jnp.exp2(scores(j, h) + jnp.where(picked, 0.0, NEG))
                    ls[h] = ls[h] + jnp.sum(p, axis=0, keepdims=True)
                    p_ref[u, h] = p.astype(BF16)
            for u in range(2):
                for h in range(heads):
                    acc_ref[h] += jnp.dot(values(2 * jj + u, h), p_ref[u, h], preferred_element_type=F32)
            return tuple(ls)

        return lax.fori_loop(0, (i + 1) // 2, body, tuple(ls))

    def running_max():
        m0, l0 = [], []
        for h in range(heads):
            s = jnp.where(causal, scores(i, h), NEG)
            m0.append(jnp.max(s, axis=0, keepdims=True))
            s_ref[h] = s
        for h in range(heads):
            p = jnp.exp2(s_ref[h] - m0[h])
            l0.append(jnp.sum(p, axis=0, keepdims=True))
            p_ref[0, h] = p.astype(BF16)
        for h in range(heads):
            acc_ref[h] = jnp.dot(values(i, h), p_ref[0, h], preferred_element_type=F32)

        def body(j, carry):
            ms, ls = carry
            tile_max, new_m, new_l, alphas = [], [], [], []
            for h in range(heads):
                s = scores(j, h)
                tile_max.append(jnp.max(s, axis=0, keepdims=True))
                s_ref[h] = s
            for h in range(heads):
                picked = sel_ref[h, pl.ds(j, 1), :] > 0.0
                m = jnp.maximum(ms[h], jnp.where(picked, tile_max[h], NEG))
                p = jnp.exp2(s_ref[h] - jnp.where(picked, m, -NEG))
                alpha = jnp.exp2(ms[h] - m)
                new_m.append(m)
                new_l.append(alpha * ls[h] + jnp.sum(p, axis=0, keepdims=True))
                alphas.append(alpha)
                p_ref[0, h] = p.astype(BF16)
            for h in range(heads):
                acc_ref[h] = alphas[h] * acc_ref[h] + jnp.dot(values(j, h), p_ref[0, h],
                                                              preferred_element_type=F32)
            return tuple(new_m), tuple(new_l)

        return lax.fori_loop(0, i, body, (tuple(m0), tuple(l0)))[1]

    ls = lax.cond(bounded_ref[0] != 0, bounded_scores, running_max)
    for pp in range(ATTN_PAIRS_PER_STEP):
        out_t = jnp.concatenate([acc_ref[2 * pp] * (1.0 / ls[2 * pp]),
                                 acc_ref[2 * pp + 1] * (1.0 / ls[2 * pp + 1])], axis=0)
        o_ref[:, pp * LANES:(pp + 1) * LANES] = out_t.T.astype(o_ref.dtype)


def _moba_attention(bounded, qn, kn, sel, vt5):
    b, s, width = qn.shape
    nb = s // MOBA_BLOCK
    pps = ATTN_PAIRS_PER_STEP
    gw = pps * LANES
    qtile = pl.BlockSpec((None, MOBA_BLOCK, gw), lambda bb, g, i: (bb, i, g))
    return pl.pallas_call(
        _attn_kernel,
        grid=(b, HEAD_PAIRS // pps, nb),
        in_specs=[
            pl.BlockSpec(memory_space=pltpu.SMEM),
            qtile,
            pl.BlockSpec((None, s, gw), lambda bb, g, i: (bb, 0, g), pipeline_mode=pl.Buffered(1)),
            pl.BlockSpec((None, pps, nb, LANES, MOBA_BLOCK), lambda bb, g, i: (bb, g, 0, 0, 0),
                         pipeline_mode=pl.Buffered(1)),
            pl.BlockSpec((None, 2 * pps, None, MAX_BLOCKS, MOBA_BLOCK), lambda bb, g, i: (bb, g, i, 0, 0)),
        ],
        out_specs=qtile,
        out_shape=jax.ShapeDtypeStruct((b, s, width), BF16),
        scratch_shapes=[
            pltpu.VMEM((2 * pps, MOBA_BLOCK, LANES), BF16),
            pltpu.VMEM((2 * pps, HEAD_DIM, MOBA_BLOCK), F32),
            pltpu.VMEM((2 * pps, MOBA_BLOCK, MOBA_BLOCK), F32),
            pltpu.VMEM((2, 2 * pps, MOBA_BLOCK, MOBA_BLOCK), BF16),
        ],
        compiler_params=pltpu.CompilerParams(
            dimension_semantics=("parallel", "parallel", "arbitrary"), vmem_limit_bytes=VMEM_LIMIT),
        name="moba_attn",
    )(bounded, qn, kn, vt5, sel)


def _ssd_kernel(xbc_ref, z_ref, dt_ref, cw_ref, cb_ref, dtb_ref, alog_ref, dsk_ref, ng_ref, ec_ref, ew_ref,
                o_ref, ext_ref, act_ref, state_ref, *chunk_scratch):
    @pl.when(pl.program_id(1) == 0)
    def _():
        ext_ref[:, 0:CONV_PAD, :] = jnp.zeros((ext_ref.shape[0], CONV_PAD, LANES), F32)
        state_ref[...] = jnp.zeros_like(state_ref)

    for sub in range(SSM_CHUNKS_PER_STEP):
        rows = pl.ds(sub * SSM_CHUNK, SSM_CHUNK)
        _ssd_chunk(xbc_ref.at[rows], z_ref.at[rows], dt_ref.at[rows], cw_ref, cb_ref, dtb_ref, alog_ref, dsk_ref,
                   ng_ref, ec_ref, ew_ref, o_ref.at[rows], ext_ref, act_ref, state_ref, *chunk_scratch)


def _ssd_chunk(xbc_ref, z_ref, dt_ref, cw_ref, cb_ref, dtb_ref, alog_ref, dsk_ref, ng_ref, ec_ref, ew_ref,
               o_ref, ext_ref, act_ref, state_ref, y_ref, acst_ref, ldtt_ref, rhs_ref, ly_ref, ls_ref):
    L = SSM_CHUNK
    b_blk = SSM_PAIRS
    c_blk = SSM_PAIRS + SSM_GROUPS

    def conv_silu(blk):
        ext_ref[blk, CONV_PAD:CONV_PAD + L, :] = xbc_ref[:, blk * LANES:(blk + 1) * LANES].astype(F32)
        w = cw_ref[blk]
        conv = cb_ref[blk]
        for k in range(SSM_CONV):
            start = CONV_PAD - (SSM_CONV - 1) + k
            conv = conv + w[k:k + 1, :] * ext_ref[blk, start:start + L, :]
        ext_ref[blk, 0:CONV_PAD, :] = ext_ref[blk, L:L + CONV_PAD, :]
        act_ref[blk] = conv * _sigmoid(conv)

    dt_in = dt_ref[...] + dtb_ref[...]
    dtv = jnp.maximum(dt_in, 0.0) + jnp.log(1.0 + jnp.exp(-jnp.abs(dt_in)))
    a2 = dtv * (-LOG2E * jnp.exp(alog_ref[...]))
    row = lax.broadcasted_iota(jnp.int32, (L, L), 0)
    col = lax.broadcasted_iota(jnp.int32, (L, L), 1)
    tri = col <= row
    acs = jnp.dot(tri.astype(F32), a2, preferred_element_type=F32, precision=lax.Precision.HIGHEST)
    acst_ref[...] = acs.T
    ldtt_ref[...] = (jnp.log(dtv) * LOG2E).T
    lane = lax.broadcasted_iota(jnp.int32, (1, LANES), 1)
    is_a = lane < HEAD_DIM
    zero = jnp.zeros((), BF16)

    def split3(v):
        hi = v.astype(BF16)
        r1 = v - hi.astype(F32)
        mid = r1.astype(BF16)
        lo = (r1 - mid.astype(F32)).astype(BF16)
        return jnp.where(lane < SSM_HEADS, hi, jnp.where(lane < 2 * SSM_HEADS, mid, lo))

    acs_parts = split3(acs)
    wst_parts = split3(jnp.exp2(acs))
    ppg = SSM_PAIRS // SSM_GROUPS

    for g in range(SSM_GROUPS):
        for blk in list(range(g * ppg, (g + 1) * ppg)) + [b_blk + g, c_blk + g]:
            conv_silu(blk)
        bg = act_ref[b_blk + g]
        cbm = _nt_dot(act_ref[c_blk + g].astype(BF16), bg.astype(BF16))
        bt = bg.T
        cmat = jnp.dot(acs_parts, ec_ref[g], preferred_element_type=F32)
        wst = jnp.dot(wst_parts, ew_ref[g], preferred_element_type=F32)
        prev = jnp.concatenate([state_ref[g * ppg + pj] for pj in range(ppg)], axis=1).astype(BF16)
        y_off = jnp.dot(act_ref[c_blk + g].astype(BF16), prev, preferred_element_type=F32) * wst
        decs = []
        for pj in range(ppg):
            xp = act_ref[g * ppg + pj].astype(BF16)
            rhs_ref[g * ppg + pj] = jnp.concatenate([jnp.where(is_a, xp, zero), jnp.where(is_a, zero, xp)], axis=0)
            lhs_y, lhs_s, dec = [], [], []
            for hh in range(2):
                k = 2 * pj + hh
                h = g * 2 * ppg + k
                arow = acst_ref[h:h + 1, :]
                rrow = arow - ldtt_ref[h:h + 1, :]
                expo = jnp.where(tri, cmat[:, k * LANES:(k + 1) * LANES] - rrow, NEG)
                lhs_y.append((jnp.exp2(expo) * cbm).astype(BF16))
                tot = arow[:, L - 1:L]
                lhs_s.append((bt * jnp.exp2(tot - rrow)).astype(BF16))
                dec.append(jnp.exp2(tot))
            ly_ref[g * ppg + pj] = jnp.concatenate(lhs_y, axis=1)
            ls_ref[g * ppg + pj] = jnp.concatenate(lhs_s, axis=1)
            decs.append(jnp.where(is_a, dec[0], dec[1]))
        for pj in range(ppg):
            y_ref[g * ppg + pj] = (jnp.dot(ly_ref[g * ppg + pj], rhs_ref[g * ppg + pj], preferred_element_type=F32)
                                   + y_off[:, pj * LANES:(pj + 1) * LANES])
        for pj in range(ppg):
            j = g * ppg + pj
            state_ref[j] = state_ref[j] * decs[pj] + jnp.dot(ls_ref[g * ppg + pj], rhs_ref[g * ppg + pj],
                                                            preferred_element_type=F32)

        ys = []
        ssq = jnp.zeros((L, 1), F32)
        for j in range(g * ppg, (g + 1) * ppg):
            cols = slice(j * LANES, (j + 1) * LANES)
            zf = z_ref[:, cols].astype(F32)
            y = (y_ref[j] + dsk_ref[:, cols] * act_ref[j]) * (zf * _sigmoid(zf))
            ssq = ssq + jnp.sum(y * y, axis=-1, keepdims=True)
            ys.append(y)
        scale = lax.rsqrt(ssq * (1.0 / (ppg * LANES)) + EPS)
        for j, y in zip(range(g * ppg, (g + 1) * ppg), ys):
            cols = slice(j * LANES, (j + 1) * LANES)
            o_ref[:, cols] = (y * scale * ng_ref[:, cols]).astype(o_ref.dtype)


def _head_expansion(width):
    r = lax.broadcasted_iota(jnp.int32, (LANES, SSM_HEADS * width), 0)
    col = lax.broadcasted_iota(jnp.int32, (LANES, SSM_HEADS * width), 1)
    e = jnp.logical_and(r < 3 * SSM_HEADS, r % SSM_HEADS == col // width).astype(BF16)
    return e.reshape(LANES, SSM_GROUPS, -1).transpose(1, 0, 2)


def _ssd(proj3, dt3, conv_w, conv_b, dt_bias, a_log, d_skip_x, norm_g):
    b, s, _ = proj3.shape
    L = SSM_CHUNK
    inner = SSM_HEADS * HEAD_DIM
    conv_dim = inner + 2 * SSM_GROUPS * SSM_STATE
    n_blk = conv_dim // LANES
    hpg = SSM_HEADS // SSM_GROUPS
    ppg = SSM_PAIRS // SSM_GROUPS
    rows = SSM_CHUNKS_PER_STEP * L
    vec = lambda w: pl.BlockSpec((1, w), lambda bb, c: (0, 0))
    return pl.pallas_call(
        _ssd_kernel,
        grid=(b, s // rows),
        in_specs=[
            pl.BlockSpec((None, rows, conv_dim), lambda bb, c: (bb, c, 1)),
            pl.BlockSpec((None, rows, inner), lambda bb, c: (bb, c, 3)),
            pl.BlockSpec((None, rows, LANES), lambda bb, c: (bb, c, 0)),
            pl.BlockSpec((n_blk, SSM_CONV, LANES), lambda bb, c: (0, 0, 0)),
            pl.BlockSpec((n_blk, 1, LANES), lambda bb, c: (0, 0, 0)),
            vec(LANES), vec(LANES), vec(inner), vec(inner),
            pl.BlockSpec((SSM_GROUPS, LANES, hpg * L), lambda bb, c: (0, 0, 0)),
            pl.BlockSpec((SSM_GROUPS, LANES, hpg * HEAD_DIM), lambda bb, c: (0, 0, 0)),
        ],
        out_specs=pl.BlockSpec((None, rows, inner), lambda bb, c: (bb, c, 0)),
        out_shape=jax.ShapeDtypeStruct((b, s, inner), BF16),
        scratch_shapes=[
            pltpu.VMEM((n_blk, L + CONV_PAD, LANES), F32),
            pltpu.VMEM((n_blk, L, LANES), F32),
            pltpu.VMEM((SSM_PAIRS, SSM_STATE, LANES), F32),
            pltpu.VMEM((SSM_PAIRS, L, LANES), F32),
            pltpu.VMEM((LANES, L), F32),
            pltpu.VMEM((LANES, L), F32),
            pltpu.VMEM((SSM_PAIRS, 2 * L, LANES), BF16),
            pltpu.VMEM((SSM_PAIRS, L, 2 * L), BF16),
            pltpu.VMEM((SSM_PAIRS, SSM_STATE, 2 * L), BF16),
        ],
        compiler_params=pltpu.CompilerParams(
            dimension_semantics=("parallel", "arbitrary"), vmem_limit_bytes=VMEM_LIMIT),
        name="ssd",
    )(proj3, proj3, dt3, conv_w, conv_b, dt_bias, a_log, d_skip_x, norm_g,
      _head_expansion(L), _head_expansion(HEAD_DIM))


def _rms(x, g):
    return x * lax.rsqrt(jnp.mean(x * x, axis=-1, keepdims=True) + EPS) * g


def _tail_kernel(att_ref, yn_ref, ga_ref, gb_ref, x_ref, woa_ref, wos_ref, wout_ref,
                 g2_ref, wgu_ref, wd_ref, p_ref, g3_ref, wpg_ref, wpp_ref, o_ref):
    ya = jnp.dot(att_ref[...], woa_ref[...], preferred_element_type=F32)
    yb = jnp.dot(yn_ref[...], wos_ref[...], preferred_element_type=F32)
    merged = _sigmoid(ga_ref[...].astype(F32)) * ya + _sigmoid(gb_ref[...].astype(F32)) * yb
    x = x_ref[...] + jnp.dot(merged.astype(BF16), wout_ref[...], preferred_element_type=F32)
    d_ff = wd_ref.shape[0]
    h = _rms(x, g2_ref[...]).astype(BF16)
    gate = jnp.dot(h, wgu_ref[:, :d_ff], preferred_element_type=F32)
    up = jnp.dot(h, wgu_ref[:, d_ff:], preferred_element_type=F32)
    act = (gate * _sigmoid(gate) * up).astype(BF16)
    x2 = x + jnp.dot(act, wd_ref[...], preferred_element_type=F32)
    h3 = _rms(x2, g3_ref[...]).astype(BF16)
    pgate = _sigmoid(jnp.dot(h3, wpg_ref[...], preferred_element_type=F32))
    proj = jnp.dot(p_ref[...].astype(BF16), wpp_ref[...], preferred_element_type=F32)
    o_ref[...] = x2 + proj * pgate


def _mix_ffn_ple(att2d, yn2d, proj2d, x2d, w_oa, w_os, w_out, g2, w_gu, w_down, p2d, g3, w_pg, w_pp, tm=512):
    m, d = x2d.shape
    ga_col = (proj2d.shape[1] - 2 * d) // d
    row = lambda w, col=0: pl.BlockSpec((tm, w), lambda i, col=col: (i, col))
    vec = lambda a: pl.BlockSpec(a.shape, lambda i: (0, 0))
    resident = lambda a: pl.BlockSpec(a.shape, lambda i: (0, 0), pipeline_mode=pl.Buffered(1))
    return pl.pallas_call(
        _tail_kernel,
        grid=(m // tm,),
        in_specs=[row(att2d.shape[1]), row(yn2d.shape[1]), row(d, ga_col), row(d, ga_col + 1), row(d),
                  resident(w_oa), resident(w_os), resident(w_out),
                  vec(g2), resident(w_gu), resident(w_down), row(p2d.shape[1]),
                  vec(g3), resident(w_pg), resident(w_pp)],
        out_specs=row(d),
        out_shape=jax.ShapeDtypeStruct((m, d), F32),
        compiler_params=pltpu.CompilerParams(
            dimension_semantics=("parallel",), vmem_limit_bytes=VMEM_LIMIT),
        name="mix_ffn_ple",
    )(att2d, yn2d, proj2d, proj2d, x2d, w_oa, w_os, w_out, g2, w_gu, w_down, p2d, g3, w_pg, w_pp)


def _head_lanes(v):
    return jnp.pad(jnp.tile(v.astype(F32), 3), (0, LANES - 3 * SSM_HEADS)).reshape(1, LANES)


def _lane_blocks(a):
    r, w = a.shape
    return a.astype(F32).reshape(r, w // LANES, LANES).transpose(1, 0, 2)


def kernel(x, p, ln1_g, w_in, q_norm_g, k_norm_g, w_o_attn, conv_w, conv_b, dt_bias, a_log, d_skip,
           ssm_norm_g, w_o_ssm, w_out, ln2_g, w_gate_up, w_down, ln3_g, w_ple_gate, w_ple_proj):
    b, s, d = x.shape
    assert s % MOBA_BLOCK == 0 and s // MOBA_BLOCK <= MAX_BLOCKS
    attn_w = ATTN_HEADS * HEAD_DIM
    inner = SSM_HEADS * HEAD_DIM
    conv_dim = inner + 2 * SSM_GROUPS * SSM_STATE
    off_z = 3 * attn_w
    off_xbc = off_z + inner
    off_dt = off_xbc + conv_dim
    off_g = off_dt + SSM_HEADS
    row = lambda v: v.astype(F32).reshape(1, -1)

    for i in range(w_in.shape[0]):
        wi = w_in[i]
        tn = 1024
        row_starts = [r for lo, hi in ((0, off_z), (off_xbc, off_dt), (off_z, off_xbc), (off_g, wi.shape[1]))
                      for r in range(lo, hi, tn)]
        w_t = wi.T
        w_dt = jnp.pad(jnp.tile(wi[:, off_dt:off_g], (1, 3)), ((0, 0), (0, LANES - 3 * SSM_HEADS))).astype(BF16)

        x2d = x.reshape(b * s, d)
        proj, dt_raw = _in_proj(x2d, row(ln1_g[i]), w_t, row_starts, w_dt, tn=tn)
        proj3 = proj.reshape(b, s, -1)

        qn, kn, sel, vt5 = _attn_prep(
            proj3, row(jnp.tile(q_norm_g[i], ATTN_HEADS)), row(jnp.tile(k_norm_g[i], ATTN_HEADS)))
        score_bound = 1.02 * LOG2E * HEAD_DIM ** 0.5 * jnp.max(jnp.abs(q_norm_g[i])) * jnp.max(jnp.abs(k_norm_g[i]))
        bounded = (score_bound <= SAFE_EXP2_RANGE).astype(jnp.int32).reshape(1)
        att = _moba_attention(bounded, qn, kn, sel, vt5)

        yn = _ssd(proj3, dt_raw.reshape(b, s, LANES), _lane_blocks(conv_w[i]), _lane_blocks(row(conv_b[i])),
                  _head_lanes(dt_bias[i]), _head_lanes(a_log[i]), row(jnp.repeat(d_skip[i], HEAD_DIM)),
                  row(ssm_norm_g[i]))

        x2 = _mix_ffn_ple(att.reshape(b * s, attn_w), yn.reshape(b * s, inner), proj, x2d,
                          w_o_attn[i].astype(BF16), w_o_ssm[i].astype(BF16), w_out[i].astype(BF16),
                          row(ln2_g[i]), w_gate_up[i].astype(BF16), w_down[i].astype(BF16),
                          p[i].reshape(b * s, -1), row(ln3_g[i]), w_ple_gate[i].astype(BF16),
                          w_ple_proj[i].astype(BF16))
        x = x2.reshape(b, s, d)
    return x
```
